```python
import math
import jax, jax.numpy as jnp
from jax import lax
import numpy as np

D_MODEL = 1024
BATCH = 4
SEQ = 8192
DEPTH = 2

N_HEADS = 16
HEAD_DIM = D_MODEL // N_HEADS
CONV_WIDTH = 31
D_FF = ((8 * D_MODEL // 3) + 127) // 128 * 128
N_EXPERTS = 8
TOP_K = 2
Q_BLOCK = 128
EPS = 1e-6
N_A_LAYERS = (DEPTH + 1) // 2
N_B_LAYERS = DEPTH - N_A_LAYERS
N_DENSE = (DEPTH + 1) // 2
N_MOE = DEPTH // 2
FORGET_BIAS_MEAN = 2.0

kernel_name = "yoco_conformer_fox_moe_block"


def rms_norm(x, g):
    xf = x.astype(jnp.float32)
    y = xf * lax.rsqrt(jnp.mean(xf * xf, axis=-1, keepdims=True) + EPS)
    return (y * g.astype(jnp.float32)).astype(x.dtype)


def layer_norm(x, g, b):
    xf = x.astype(jnp.float32)
    mu = jnp.mean(xf, axis=-1, keepdims=True)
    var = jnp.mean(jnp.square(xf - mu), axis=-1, keepdims=True)
    y = (xf - mu) * lax.rsqrt(var + EPS)
    return (y * g.astype(jnp.float32) + b.astype(jnp.float32)).astype(x.dtype)


def conformer_conv(h, w_in, b_in, w_dw, b_dw, ln_g, ln_b, w_out, b_out):
    u = h @ w_in + b_in
    a, gate = jnp.split(u, 2, axis=-1)
    u = a * jax.nn.sigmoid(gate)
    u = lax.conv_general_dilated(
        u, w_dw[:, None, :].astype(u.dtype),
        window_strides=(1,), padding=[(CONV_WIDTH - 1, 0)],
        dimension_numbers=("NWC", "WIO", "NWC"),
        feature_group_count=D_MODEL) + b_dw
    u = layer_norm(u, ln_g, ln_b)
    u = jax.nn.silu(u)
    return u @ w_out + b_out


def swiglu(h, w_gate, w_up, w_down):
    return (jax.nn.silu(h @ w_gate) * (h @ w_up)) @ w_down


def moe_swiglu(h, router_w, router_b, w_gate, w_up, w_down):
    logits = (h @ router_w + router_b).astype(jnp.float32)
    top_vals, top_idx = lax.top_k(logits, TOP_K)
    top_p = jax.nn.softmax(top_vals, axis=-1)
    gates = jnp.sum(jax.nn.one_hot(top_idx, N_EXPERTS, dtype=jnp.float32) * top_p[..., None], axis=-2)
    gates = gates.astype(h.dtype)
    out = jnp.zeros_like(h)
    for e in range(N_EXPERTS):
        out = out + gates[..., e:e + 1] * swiglu(h, w_gate[e], w_up[e], w_down[e])
    return out


def shared_kv(x, kv_norm, w_kvf, b_f):
    bsz, seq, _ = x.shape
    hn = rms_norm(x, kv_norm)
    proj = hn @ w_kvf
    k = proj[..., :D_MODEL].reshape(bsz, seq, N_HEADS, HEAD_DIM)
    v = proj[..., D_MODEL:2 * D_MODEL].reshape(bsz, seq, N_HEADS, HEAD_DIM)
    f_logit = proj[..., 2 * D_MODEL:].astype(jnp.float32) + b_f.astype(jnp.float32)
    logf_cum = jnp.cumsum(jax.nn.log_sigmoid(f_logit), axis=1)
    return k, v, jnp.transpose(logf_cum, (0, 2, 1))


def forgetting_attention(q, k, v, logf_cum):
    seq = q.shape[1]
    scale = HEAD_DIM ** -0.5
    qf = q.astype(jnp.float32)
    kf = k.astype(jnp.float32)
    outs = []
    for blk in range(seq // Q_BLOCK):
        q0 = blk * Q_BLOCK
        q1 = q0 + Q_BLOCK
        s = jnp.einsum("bqhd,bkhd->bhqk", qf[:, q0:q1], kf[:, :q1]) * scale
        s = s + logf_cum[:, :, q0:q1, None] - logf_cum[:, :, None, :q1]
        mask = (q0 + jnp.arange(Q_BLOCK))[:, None] >= jnp.arange(q1)[None, :]
        s = jnp.where(mask, s, -jnp.inf)
        p = jax.nn.softmax(s, axis=-1).astype(v.dtype)
        outs.append(jnp.einsum("bhqk,bkhd->bqhd", p, v[:, :q1]))
    return jnp.concatenate(outs, axis=1)


def setup_inputs(seed: int = 0) -> dict:
    key = jax.random.key(seed)
    ks = jax.random.split(key, 32)
    f32 = jnp.float32

    def w(k, shape, fan_in):
        return jax.random.normal(k, shape, f32) * (fan_in ** -0.5)

    def gain(k, shape):
        return 1.0 + 0.02 * jax.random.normal(k, shape, f32)

    def bias(k, shape, scale=0.02):
        return scale * jax.random.normal(k, shape, f32)

    D = D_MODEL
    return {
        "x": jax.random.normal(ks[0], (BATCH, SEQ, D), f32),
        "mix_norm": gain(ks[1], (DEPTH, D)),
        "ffn_norm": gain(ks[2], (DEPTH, D)),
        "conv_w_in": w(ks[3], (N_A_LAYERS, D, 2 * D), D),
        "conv_b_in": bias(ks[4], (N_A_LAYERS, 2 * D)),
        "conv_w_dw": w(ks[5], (N_A_LAYERS, CONV_WIDTH, D), CONV_WIDTH),
        "conv_b_dw": bias(ks[6], (N_A_LAYERS, D)),
        "conv_ln_g": gain(ks[7], (N_A_LAYERS, D)),
        "conv_ln_b": bias(ks[8], (N_A_LAYERS, D)),
        "conv_w_out": w(ks[9], (N_A_LAYERS, D, D), D),
        "conv_b_out": bias(ks[10], (N_A_LAYERS, D)),
        "kv_norm": gain(ks[11], (D,)),
        "w_kvf": w(ks[12], (D, 2 * D + N_HEADS), D),
        "b_f": FORGET_BIAS_MEAN + 0.5 * jax.random.normal(ks[13], (N_HEADS,), f32),
        "w_q": w(ks[14], (N_B_LAYERS, D, D), D),
        "w_o": w(ks[15], (N_B_LAYERS, D, D), D),
        "ffn_w_gate": w(ks[16], (N_DENSE, D, D_FF), D),
        "ffn_w_up": w(ks[17], (N_DENSE, D, D_FF), D),
        "ffn_w_down": w(ks[18], (N_DENSE, D_FF, D), D_FF),
        "router_w": w(ks[19], (N_MOE, D, N_EXPERTS), D),
        "router_b": bias(ks[20], (N_MOE, N_EXPERTS), 0.01),
        "moe_w_gate": w(ks[21], (N_MOE, N_EXPERTS, D, D_FF), D),
        "moe_w_up": w(ks[22], (N_MOE, N_EXPERTS, D, D_FF), D),
        "moe_w_down": w(ks[23], (N_MOE, N_EXPERTS, D_FF, D), D_FF),
        "final_norm": gain(ks[24], (D,)),
    }


def reference(x, mix_norm, ffn_norm, conv_w_in, conv_b_in, conv_w_dw, conv_b_dw, conv_ln_g, conv_ln_b,
              conv_w_out, conv_b_out, kv_norm, w_kvf, b_f, w_q, w_o, ffn_w_gate, ffn_w_up, ffn_w_down,
              router_w, router_b, moe_w_gate, moe_w_up, moe_w_down, final_norm):
    bsz, seq, _ = x.shape
    a_i = b_i = d_i = m_i = 0
    k = v = logf_cum = None
    for layer in range(DEPTH):
        h = rms_norm(x, mix_norm[layer])
        if layer < N_A_LAYERS:
            x = x + conformer_conv(h, conv_w_in[a_i], conv_b_in[a_i], conv_w_dw[a_i], conv_b_dw[a_i],
                                   conv_ln_g[a_i], conv_ln_b[a_i], conv_w_out[a_i], conv_b_out[a_i])
            a_i += 1
        else:
            if k is None:
                k, v, logf_cum = shared_kv(x, kv_norm, w_kvf, b_f)
            q = (h @ w_q[b_i]).reshape(bsz, seq, N_HEADS, HEAD_DIM)
            o = forgetting_attention(q, k, v, logf_cum).reshape(bsz, seq, D_MODEL)
            x = x + o @ w_o[b_i]
            b_i += 1
        h = rms_norm(x, ffn_norm[layer])
        if layer % 2 == 0:
            x = x + swiglu(h, ffn_w_gate[d_i], ffn_w_up[d_i], ffn_w_down[d_i])
            d_i += 1
        else:
            x = x + moe_swiglu(h, router_w[m_i], router_b[m_i], moe_w_gate[m_i], moe_w_up[m_i], moe_w_down[m_i])
            m_i += 1
    return rms_norm(x, final_norm)
```

```python
import functools

import jax
import jax.numpy as jnp
from jax import lax
from jax.experimental import pallas as pl
from jax.experimental.pallas import tpu as pltpu

F32 = jnp.float32
BF16 = jnp.bfloat16

EPS = 1e-6
N_HEADS = 16
HEAD_DIM = 64
CONV_WIDTH = 31
TOP_K = 2

LANES = 128
SUBLANES = 8
MIB = 1024 * 1024

HALO = 32
CONV_ROWS = 64
CONV_COLS = 256
AUG = 2 * HEAD_DIM
V_ROWS = HEAD_DIM + 16
NEG_BIG = -1e30


def _rms(x, g):
    return x * lax.rsqrt(jnp.mean(x * x, axis=-1, keepdims=True) + EPS) * g


def _dot(a, b):
    return jnp.dot(a, b, preferred_element_type=F32)


def _dot_nt(a, b):
    return lax.dot_general(a, b, (((1,), (1,)), ((), ())), preferred_element_type=F32)


def _dot_exact(a, b):
    return jnp.dot(a, b, preferred_element_type=F32, precision=lax.Precision.HIGHEST)


def _params(sem, vmem_mib):
    return pltpu.CompilerParams(dimension_semantics=sem, vmem_limit_bytes=vmem_mib * MIB)


def _mixer_kernel(x_ref, g_ref, win_ref, bin_ref, wdw_ref, bdw_ref, lng_ref, lnb_ref, wout_ref, bout_ref,
                  o_ref, ubuf, shifted, cbuf):
    tm, d = x_ref.shape

    @pl.when(pl.program_id(1) == 0)
    def _():
        ubuf[0:HALO, :] = jnp.zeros((HALO, d), F32)

    x = x_ref[...]
    h = _rms(x, g_ref[...]).astype(BF16)
    u = _dot(h, win_ref[...]) + bin_ref[...]
    ubuf[HALO:HALO + tm, :] = u[:, :d] * jax.nn.sigmoid(u[:, d:])

    n_sh = tm + HALO - SUBLANES
    for b in range(1, SUBLANES):
        shifted[b - 1, 0:n_sh, :] = ubuf[b:b + n_sh, :]

    first = HALO - (CONV_WIDTH - 1)

    def conv_rows(i, carry):
        r0 = pl.multiple_of(i * CONV_ROWS, CONV_ROWS)
        for c in range(d // CONV_COLS):
            cols = slice(c * CONV_COLS, (c + 1) * CONV_COLS)
            acc = jnp.zeros((CONV_ROWS, CONV_COLS), F32)
            for k in range(CONV_WIDTH):
                off = first + k
                a8, b = (off // SUBLANES) * SUBLANES, off % SUBLANES
                if b == 0:
                    win = ubuf[pl.ds(r0 + a8, CONV_ROWS), cols]
                else:
                    win = shifted[b - 1, pl.ds(r0 + a8, CONV_ROWS), cols]
                acc = acc + win * wdw_ref[k:k + 1, cols]
            cbuf[pl.ds(r0, CONV_ROWS), cols] = acc
        return carry

    lax.fori_loop(0, tm // CONV_ROWS, conv_rows, 0)

    ubuf[0:HALO, :] = ubuf[tm:tm + HALO, :]

    c = cbuf[...] + bdw_ref[...]
    mu = jnp.mean(c, axis=-1, keepdims=True)
    cc = c - mu
    var = jnp.mean(cc * cc, axis=-1, keepdims=True)
    y = cc * lax.rsqrt(var + EPS) * lng_ref[...] + lnb_ref[...]
    y = (y * jax.nn.sigmoid(y)).astype(BF16)
    o_ref[...] = x + _dot(y, wout_ref[...]) + bout_ref[...]


def _mixer0(x, g, w_in, b_in, w_dw, b_dw, ln_g, ln_b, w_out, b_out, tm=512):
    bsz, seq, d = x.shape
    row = lambda a: a.reshape(1, -1).astype(F32)
    const = lambda shape: pl.BlockSpec(shape, lambda b, s: (0,) * len(shape))
    return pl.pallas_call(
        _mixer_kernel,
        grid=(bsz, seq // tm),
        in_specs=[
            pl.BlockSpec((None, tm, d), lambda b, s: (b, s, 0)),
            const((1, d)), const((d, 2 * d)), const((1, 2 * d)), const((CONV_WIDTH, d)), const((1, d)),
            const((1, d)), const((1, d)), const((d, d)), const((1, d)),
        ],
        out_specs=pl.BlockSpec((None, tm, d), lambda b, s: (b, s, 0)),
        out_shape=jax.ShapeDtypeStruct((bsz, seq, d), F32),
        scratch_shapes=[
            pltpu.VMEM((tm + HALO, d), F32),
            pltpu.VMEM((SUBLANES - 1, tm + HALO - SUBLANES, d), F32),
            pltpu.VMEM((tm, d), F32),
        ],
        compiler_params=_params(("parallel", "arbitrary"), 56),
        name="mixer0",
    )(x, row(g), w_in.astype(BF16), row(b_in), w_dw.astype(F32), row(b_dw), row(ln_g), row(ln_b),
      w_out.astype(BF16), row(b_out))


def _swiglu(h, wg_ref, wu_ref, wd_ref, n_chunks):
    f = wg_ref.shape[-1]
    fc = f // n_chunks
    y = None
    for c in range(n_chunks):
        cols = slice(c * fc, (c + 1) * fc)
        g = _dot(h, wg_ref[:, cols])
        u = _dot(h, wu_ref[:, cols])
        a = (g * jax.nn.sigmoid(g) * u).astype(BF16)
        part = _dot(a, wd_ref[cols, :])
        y = part if y is None else y + part
    return y


def _ffn_kernel(x_ref, g_ref, wg_ref, wu_ref, wd_ref, o_ref):
    x = x_ref[...]
    h = _rms(x, g_ref[...]).astype(BF16)
    o_ref[...] = x + _swiglu(h, wg_ref, wu_ref, wd_ref, 2)


def _ffn0(x, g, w_gate, w_up, w_down, tm=512):
    t, d = x.shape
    f = w_gate.shape[-1]
    const = lambda shape: pl.BlockSpec(shape, lambda i: (0,) * len(shape), pipeline_mode=pl.Buffered(1))
    return pl.pallas_call(
        _ffn_kernel,
        grid=(t // tm,),
        in_specs=[pl.BlockSpec((tm, d), lambda i: (i, 0)), const((1, d)), const((d, f)), const((d, f)),
                  const((f, d))],
        out_specs=pl.BlockSpec((tm, d), lambda i: (i, 0)),
        out_shape=jax.ShapeDtypeStruct((t, d), F32),
        compiler_params=_params(("parallel",), 56),
        name="ffn0",
    )(x, g.reshape(1, d).astype(F32), w_gate.astype(BF16), w_up.astype(BF16), w_down.astype(BF16))


def _qkv_kernel(x_ref, gq_ref, gkv_ref, wq_ref, wk_ref, wvt_ref, wf_ref, bf_ref, selq_ref, selk_ref, cq_ref,
                ck_ref, qa_ref, ka_ref, vt_ref, carry):
    tm, d = x_ref.shape

    @pl.when(pl.program_id(1) == 0)
    def _():
        carry[...] = jnp.zeros_like(carry)

    x = x_ref[...]
    xn = x * lax.rsqrt(jnp.mean(x * x, axis=-1, keepdims=True) + EPS)
    hq = (xn * gq_ref[...]).astype(BF16)
    hkv32 = xn * gkv_ref[...]
    hkv = hkv32.astype(BF16)

    f_logit = _dot_exact(hkv32, wf_ref[...]) + bf_ref[...]
    logsig = jnp.minimum(f_logit, 0.0) - jnp.log1p(jnp.exp(-jnp.abs(f_logit)))
    r_i = lax.broadcasted_iota(jnp.int32, (tm, tm), 0)
    c_i = lax.broadcasted_iota(jnp.int32, (tm, tm), 1)
    tri = (c_i <= r_i).astype(F32)
    dcum = _dot_exact(tri, logsig) + carry[0:1, :]
    carry[0:1, :] = dcum[tm - 1:tm, :]

    lane = lax.broadcasted_iota(jnp.int32, dcum.shape, 1)
    p0 = dcum.astype(BF16).astype(F32)
    r1 = dcum - p0
    p1 = r1.astype(BF16).astype(F32)
    p2 = (r1 - p1).astype(BF16).astype(F32)
    pieces = jnp.where(lane < N_HEADS, p0, jnp.where(lane < 2 * N_HEADS, p1, jnp.where(lane < 3 * N_HEADS, p2, 0.0)))
    pieces = pieces.astype(BF16)
    xq = _dot(pieces, selq_ref[...]) + cq_ref[...]
    xk = _dot(pieces, selk_ref[...]) + ck_ref[...]

    q = _dot(hq, wq_ref[...]) * (HEAD_DIM ** -0.5)
    k = _dot(hkv, wk_ref[...])
    lane_a = lax.broadcasted_iota(jnp.int32, (tm, AUG), 1)
    for p in range(N_HEADS // 2):
        src = slice(p * AUG, (p + 1) * AUG)
        for odd in range(2):
            dst = slice((2 * p + odd) * AUG, (2 * p + odd + 1) * AUG)
            qv, kv = q[:, src], k[:, src]
            if odd:
                qv, kv = pltpu.roll(qv, HEAD_DIM, axis=1), pltpu.roll(kv, HEAD_DIM, axis=1)
            qa_ref[:, dst] = jnp.where(lane_a < HEAD_DIM, qv, xq[:, dst]).astype(BF16)
            ka_ref[:, dst] = jnp.where(lane_a < HEAD_DIM, kv, xk[:, dst]).astype(BF16)

    vt = _dot_nt(wvt_ref[...], hkv).astype(BF16)
    row = lax.broadcasted_iota(jnp.int32, (V_ROWS - HEAD_DIM, tm), 0)
    ones_row = (row == 0).astype(BF16)
    for h in range(N_HEADS):
        vt_ref[h, 0:HEAD_DIM, :] = vt[h * HEAD_DIM:(h + 1) * HEAD_DIM, :]
        vt_ref[h, HEAD_DIM:V_ROWS, :] = ones_row


def _qkv(x, g_q, g_kv, w_q, w_kvf, b_f, tm=512):
    bsz, seq, d = x.shape
    nh = N_HEADS
    w_k = w_kvf[:, :d]
    w_vt = w_kvf[:, d:2 * d].T
    w_f = w_kvf[:, 2 * d:]
    w_f3 = jnp.zeros((d, LANES), F32).at[:, :3 * nh].set(jnp.tile(w_f, (1, 3)))
    b_f3 = jnp.zeros((1, LANES), F32).at[0, :3 * nh].set(jnp.tile(b_f.astype(F32), 3))

    j = jnp.arange(3)[:, None]
    hh = jnp.arange(nh)[None, :]
    src = (j * nh + hh).reshape(-1)
    dq = (hh * AUG + HEAD_DIM + j).reshape(-1)
    dk = (hh * AUG + HEAD_DIM + 3 + j).reshape(-1)
    selq = jnp.zeros((LANES, nh * AUG), BF16).at[src, dq].set(1)
    selk = jnp.zeros((LANES, nh * AUG), BF16).at[src, dk].set(1)
    cq = jnp.zeros((1, nh * AUG), F32).at[0, dk].set(-1.0)
    ck = jnp.zeros((1, nh * AUG), F32).at[0, dq].set(1.0)

    const = lambda shape: pl.BlockSpec(shape, lambda b, s: (0,) * len(shape))
    return pl.pallas_call(
        _qkv_kernel,
        grid=(bsz, seq // tm),
        in_specs=[
            pl.BlockSpec((None, tm, d), lambda b, s: (b, s, 0)),
            const((1, d)), const((1, d)), const((d, d)), const((d, d)), const((d, d)), const((d, LANES)),
            const((1, LANES)), const((LANES, nh * AUG)), const((LANES, nh * AUG)), const((1, nh * AUG)),
            const((1, nh * AUG)),
        ],
        out_specs=[
            pl.BlockSpec((None, tm, nh * AUG), lambda b, s: (b, s, 0)),
            pl.BlockSpec((None, tm, nh * AUG), lambda b, s: (b, s, 0)),
            pl.BlockSpec((None, nh, V_ROWS, tm), lambda b, s: (b, 0, 0, s)),
        ],
        out_shape=[
            jax.ShapeDtypeStruct((bsz, seq, nh * AUG), BF16),
            jax.ShapeDtypeStruct((bsz, seq, nh * AUG), BF16),
            jax.ShapeDtypeStruct((bsz, nh, V_ROWS, seq), BF16),
        ],
        scratch_shapes=[pltpu.VMEM((SUBLANES, LANES), F32)],
        compiler_params=_params(("parallel", "arbitrary"), 56),
        name="qkv",
    )(x, g_q.reshape(1, d).astype(F32), g_kv.reshape(1, d).astype(F32), w_q.astype(BF16), w_k.astype(BF16),
      w_vt.astype(BF16), w_f3, b_f3, selq, selk, cq, ck)


def _attn_kernel(qa_ref, ka_ref, vt_ref, o_ref):
    tq = qa_ref.shape[0]
    qi = pl.program_id(2)
    outs = []
    for hl in range(2):
        cols = slice(hl * AUG, (hl + 1) * AUG)
        q = qa_ref[:, cols]

        def scores(kb):
            k_blk = ka_ref[pl.ds(pl.multiple_of(kb * tq, tq), tq), cols]
            return _dot_nt(k_blk, q)

        def values(kb):
            return vt_ref[hl, :, pl.ds(pl.multiple_of(kb * tq, tq), tq)]

        s = scores(qi)
        key = lax.broadcasted_iota(jnp.int32, s.shape, 0)
        qry = lax.broadcasted_iota(jnp.int32, s.shape, 1)
        s = jnp.where(key <= qry, s, -jnp.inf)
        m0 = jnp.max(s, axis=0, keepdims=True)
        p = jnp.exp(s - m0).astype(BF16)
        acc0 = _dot(values(qi), p)

        def body(kb, carry):
            m, acc = carry
            s = scores(kb)
            m_new = jnp.maximum(m, jnp.max(s, axis=0, keepdims=True))
            p = jnp.exp(s - m_new).astype(BF16)
            acc = jnp.exp(m - m_new) * acc + _dot(values(kb), p)
            return m_new, acc

        _, acc = lax.fori_loop(0, qi, body, (m0, acc0))
        outs.append(acc[0:HEAD_DIM, :] / acc[HEAD_DIM:HEAD_DIM + 1, :])
    o_ref[...] = jnp.concatenate(outs, axis=0).T.astype(o_ref.dtype)


def _attention(qa, ka, vt, tq=512):
    bsz, seq, _ = qa.shape
    nh = N_HEADS
    return pl.pallas_call(
        _attn_kernel,
        grid=(bsz, nh // 2, seq // tq),
        in_specs=[
            pl.BlockSpec((None, tq, 2 * AUG), lambda b, h, i: (b, i, h)),
            pl.BlockSpec((None, seq, 2 * AUG), lambda b, h, i: (b, 0, h)),
            pl.BlockSpec((None, 2, V_ROWS, seq), lambda b, h, i: (b, h, 0, 0)),
        ],
        out_specs=pl.BlockSpec((None, tq, 2 * HEAD_DIM), lambda b, h, i: (b, i, h)),
        out_shape=jax.ShapeDtypeStruct((bsz, seq, nh * HEAD_DIM), BF16),
        compiler_params=_params(("parallel", "parallel", "arbitrary"), 48),
        name="attn",
    )(qa, ka, vt)


def _oproj_kernel(x_ref, o_ref, wo_ref, g_ref, wr_ref, br_ref, x3_ref, h_ref, gates_ref):
    x3 = x_ref[...] + _dot(o_ref[...], wo_ref[...])
    x3_ref[...] = x3
    h = _rms(x3, g_ref[...])
    h_ref[...] = h.astype(BF16)
    logits = _dot_exact(h, wr_ref[...]) + br_ref[...]
    lane = lax.broadcasted_iota(jnp.int32, logits.shape, 1)
    m1 = jnp.max(logits, axis=-1, keepdims=True)
    i1 = jnp.min(jnp.where(logits == m1, lane, LANES), axis=-1, keepdims=True)
    rest = jnp.where(lane == i1, -jnp.inf, logits)
    m2 = jnp.max(rest, axis=-1, keepdims=True)
    i2 = jnp.min(jnp.where(rest == m2, lane, LANES), axis=-1, keepdims=True)
    e = jnp.exp(m2 - m1)
    den = 1.0 + e
    gates_ref[...] = jnp.where(lane == i1, 1.0 / den, 0.0) + jnp.where(lane == i2, e / den, 0.0)


def _oproj_router(x, o, w_o, g, router_w, router_b, tm=512):
    t, d = x.shape
    ne = router_w.shape[-1]
    wr = jnp.zeros((d, LANES), F32).at[:, :ne].set(router_w.astype(F32))
    br = jnp.full((1, LANES), NEG_BIG, F32).at[0, :ne].set(router_b.astype(F32))
    const = lambda shape: pl.BlockSpec(shape, lambda i: (0,) * len(shape))
    return pl.pallas_call(
        _oproj_kernel,
        grid=(t // tm,),
        in_specs=[pl.BlockSpec((tm, d), lambda i: (i, 0)), pl.BlockSpec((tm, d), lambda i: (i, 0)),
                  const((d, d)), const((1, d)), const((d, LANES)), const((1, LANES))],
        out_specs=[pl.BlockSpec((tm, d), lambda i: (i, 0)), pl.BlockSpec((tm, d), lambda i: (i, 0)),
                   pl.BlockSpec((tm, LANES), lambda i: (i, 0))],
        out_shape=[jax.ShapeDtypeStruct((t, d), F32), jax.ShapeDtypeStruct((t, d), BF16),
                   jax.ShapeDtypeStruct((t, LANES), F32)],
        compiler_params=_params(("parallel",), 40),
        name="oproj_router",
    )(x, o, w_o.astype(BF16), g.reshape(1, d).astype(F32), wr, br)


def _moe_kernel(x_ref, h_ref, gates_ref, wg_ref, wu_ref, wd_ref, gf_ref, o_ref, acc):
    e = pl.program_id(1)

    @pl.when(e == 0)
    def _():
        acc[...] = x_ref[...]

    gates = gates_ref[...]
    lane = lax.broadcasted_iota(jnp.int32, gates.shape, 1)
    gate = jnp.sum(jnp.where(lane == e, gates, 0.0), axis=-1, keepdims=True)
    acc[...] += gate * _swiglu(h_ref[...], wg_ref, wu_ref, wd_ref, 2)

    @pl.when(e == pl.num_programs(1) - 1)
    def _():
        o_ref[...] = _rms(acc[...], gf_ref[...])


def _moe_dense(x, h, gates, w_gate, w_up, w_down, g_final, tm=512):
    t, d = x.shape
    ne, _, f = w_gate.shape
    return pl.pallas_call(
        _moe_kernel,
        grid=(t // tm, ne),
        in_specs=[
            pl.BlockSpec((tm, d), lambda i, e: (i, 0)),
            pl.BlockSpec((tm, d), lambda i, e: (i, 0)),
            pl.BlockSpec((tm, LANES), lambda i, e: (i, 0)),
            pl.BlockSpec((None, d, f), lambda i, e: (e, 0, 0)),
            pl.BlockSpec((None, d, f), lambda i, e: (e, 0, 0)),
            pl.BlockSpec((None, f, d), lambda i, e: (e, 0, 0)),
            pl.BlockSpec((1, d), lambda i, e: (0, 0)),
        ],
        out_specs=pl.BlockSpec((tm, d), lambda i, e: (i, 0)),
        out_shape=jax.ShapeDtypeStruct((t, d), F32),
        scratch_shapes=[pltpu.VMEM((tm, d), F32)],
        compiler_params=_params(("parallel", "arbitrary"), 60),
        name="moe",
    )(x, h, gates, w_gate.astype(BF16), w_up.astype(BF16), w_down.astype(BF16),
      g_final.reshape(1, d).astype(F32))


def kernel(x, mix_norm, ffn_norm, conv_w_in, conv_b_in, conv_w_dw, conv_b_dw, conv_ln_g, conv_ln_b, conv_w_out,
           conv_b_out, kv_norm, w_kvf, b_f, w_q, w_o, ffn_w_gate, ffn_w_up, ffn_w_down, router_w, router_b,
           moe_w_gate, moe_w_up, moe_w_down, final_norm):
    bsz, seq, d = x.shape
    assert mix_norm.shape[0] == 2 and conv_w_in.shape[0] == 1 and w_q.shape[0] == 1 and moe_w_gate.shape[0] == 1
    assert d == N_HEADS * HEAD_DIM and conv_w_dw.shape[1] == CONV_WIDTH

    x1 = _mixer0(x, mix_norm[0], conv_w_in[0], conv_b_in[0], conv_w_dw[0], conv_b_dw[0], conv_ln_g[0],
                 conv_ln_b[0], conv_w_out[0], conv_b_out[0])
    x2 = _ffn0(x1.reshape(bsz * seq, d), ffn_norm[0], ffn_w_gate[0], ffn_w_up[0], ffn_w_down[0])
    qa, ka, vt = _qkv(x2.reshape(bsz, seq, d), mix_norm[1], kv_norm, w_q[0], w_kvf, b_f)
    o = _attention(qa, ka, vt)
    x3, h3, gates = _oproj_router(x2, o.reshape(bsz * seq, d), w_o[0], ffn_norm[1], router_w[0], router_b[0])
    out = _moe_dense(x3, h3, gates, moe_w_gate[0], moe_w_up[0], moe_w_down[0], final_norm)
    return out.reshape(bsz, seq, d)
```

```python
import functools

import jax
import jax.numpy as jnp
from jax import lax
from jax.experimental import pallas as pl
from jax.experimental.pallas import tpu as pltpu

F32 = jnp.float32
BF16 = jnp.bfloat16

EPS = 1e-6
N_HEADS = 16
HEAD_DIM = 64
CONV_WIDTH = 31
TOP_K = 2

LANES = 128
SUBLANES = 8
MIB = 1024 * 1024

HALO = 32
CONV_ROWS = 64
CONV_COLS = 256
AUG = 2 * HEAD_DIM
V_ROWS = HEAD_DIM + 16
MOE_ROWS = 512
NEG_BIG = -1e30
LOG2E = 1.4426950408889634


def _rms(x, g):
    return x * lax.rsqrt(jnp.mean(x * x, axis=-1, keepdims=True) + EPS) * g


def _dot(a, b):
    return jnp.dot(a, b, preferred_element_type=F32)


def _dot_nt(a, b):
    return lax.dot_general(a, b, (((1,), (1,)), ((), ())), preferred_element_type=F32)


def _split3(x):
    p0 = x.astype(BF16).astype(F32)
    r = x - p0
    p1 = r.astype(BF16).astype(F32)
    return p0, p1, (r - p1).astype(BF16).astype(F32)


def _hi_lo(w):
    w = jnp.pad(w.astype(F32), ((0, 0), (0, LANES - w.shape[1])))
    hi = w.astype(BF16)
    return jnp.concatenate([hi, (w - hi.astype(F32)).astype(BF16)], axis=1)


def _dot_split(x, w_ref):
    hi = x.astype(BF16)
    lo = (x - hi.astype(F32)).astype(BF16)
    both = _dot(hi, w_ref[...])
    return both[:, :LANES] + both[:, LANES:] + _dot(lo, w_ref[:, :LANES])


def _params(sem, vmem_mib):
    return pltpu.CompilerParams(dimension_semantics=sem, vmem_limit_bytes=vmem_mib * MIB)


def _mixer_kernel(x_ref, g_ref, win_ref, bin_ref, wdw_ref, bdw_ref, lng_ref, lnb_ref, wout_ref, bout_ref,
                  o_ref, ubuf, shifted, cbuf):
    tm, d = x_ref.shape

    @pl.when(pl.program_id(1) == 0)
    def _():
        ubuf[0:HALO, :] = jnp.zeros((HALO, d), F32)

    x = x_ref[...]
    h = _rms(x, g_ref[...]).astype(BF16)
    u = _dot(h, win_ref[...]) + bin_ref[...]
    ubuf[HALO:HALO + tm, :] = u[:, :d] * jax.nn.sigmoid(u[:, d:])

    n_sh = tm + HALO - SUBLANES
    for b in range(1, SUBLANES):
        shifted[b - 1, 0:n_sh, :] = ubuf[b:b + n_sh, :]

    first = HALO - (CONV_WIDTH - 1)

    def conv_rows(i, carry):
        r0 = pl.multiple_of(i * CONV_ROWS, CONV_ROWS)
        for c in range(d // CONV_COLS):
            cols = slice(c * CONV_COLS, (c + 1) * CONV_COLS)
            acc = jnp.zeros((CONV_ROWS, CONV_COLS), F32)
            for k in range(CONV_WIDTH):
                off = first + k
                a8, b = (off // SUBLANES) * SUBLANES, off % SUBLANES
                if b == 0:
                    win = ubuf[pl.ds(r0 + a8, CONV_ROWS), cols]
                else:
                    win = shifted[b - 1, pl.ds(r0 + a8, CONV_ROWS), cols]
                acc = acc + win * wdw_ref[k:k + 1, cols]
            cbuf[pl.ds(r0, CONV_ROWS), cols] = acc
        return carry

    lax.fori_loop(0, tm // CONV_ROWS, conv_rows, 0)

    ubuf[0:HALO, :] = ubuf[tm:tm + HALO, :]

    c = cbuf[...] + bdw_ref[...]
    mu = jnp.mean(c, axis=-1, keepdims=True)
    cc = c - mu
    var = jnp.mean(cc * cc, axis=-1, keepdims=True)
    y = cc * lax.rsqrt(var + EPS) * lng_ref[...] + lnb_ref[...]
    y = (y * jax.nn.sigmoid(y)).astype(BF16)
    o_ref[...] = x + _dot(y, wout_ref[...]) + bout_ref[...]


def _mixer0(x, g, w_in, b_in, w_dw, b_dw, ln_g, ln_b, w_out, b_out, tm=512):
    bsz, seq, d = x.shape
    row = lambda a: a.reshape(1, -1).astype(F32)
    const = lambda shape: pl.BlockSpec(shape, lambda b, s: (0,) * len(shape))
    return pl.pallas_call(
        _mixer_kernel,
        grid=(bsz, seq // tm),
        in_specs=[
            pl.BlockSpec((None, tm, d), lambda b, s: (b, s, 0)),
            const((1, d)), const((d, 2 * d)), const((1, 2 * d)), const((CONV_WIDTH, d)), const((1, d)),
            const((1, d)), const((1, d)), const((d, d)), const((1, d)),
        ],
        out_specs=pl.BlockSpec((None, tm, d), lambda b, s: (b, s, 0)),
        out_shape=jax.ShapeDtypeStruct((bsz, seq, d), F32),
        scratch_shapes=[
            pltpu.VMEM((tm + HALO, d), F32),
            pltpu.VMEM((SUBLANES - 1, tm + HALO - SUBLANES, d), F32),
            pltpu.VMEM((tm, d), F32),
        ],
        compiler_params=_params(("parallel", "arbitrary"), 56),
        name="mixer0",
    )(x, row(g), w_in.astype(BF16), row(b_in), w_dw.astype(F32), row(b_dw), row(ln_g), row(ln_b),
      w_out.astype(BF16), row(b_out))


def _swiglu(h, wg_ref, wu_ref, wd_ref, n_chunks):
    f = wg_ref.shape[-1]
    fc = f // n_chunks
    y = None
    for c in range(n_chunks):
        cols = slice(c * fc, (c + 1) * fc)
        g = _dot(h, wg_ref[:, cols])
        u = _dot(h, wu_ref[:, cols])
        a = (g * jax.nn.sigmoid(g) * u).astype(BF16)
        part = _dot(a, wd_ref[cols, :])
        y = part if y is None else y + part
    return y


def _ffn_kernel(x_ref, g_ref, wg_ref, wu_ref, wd_ref, o_ref):
    x = x_ref[...]
    h = _rms(x, g_ref[...]).astype(BF16)
    o_ref[...] = x + _swiglu(h, wg_ref, wu_ref, wd_ref, 2)


def _ffn0(x, g, w_gate, w_up, w_down, tm=512):
    t, d = x.shape
    f = w_gate.shape[-1]
    const = lambda shape: pl.BlockSpec(shape, lambda i: (0,) * len(shape), pipeline_mode=pl.Buffered(1))
    return pl.pallas_call(
        _ffn_kernel,
        grid=(t // tm,),
        in_specs=[pl.BlockSpec((tm, d), lambda i: (i, 0)), const((1, d)), const((d, f)), const((d, f)),
                  const((f, d))],
        out_specs=pl.BlockSpec((tm, d), lambda i: (i, 0)),
        out_shape=jax.ShapeDtypeStruct((t, d), F32),
        compiler_params=_params(("parallel",), 56),
        name="ffn0",
    )(x, g.reshape(1, d).astype(F32), w_gate.astype(BF16), w_up.astype(BF16), w_down.astype(BF16))


def _qkv_kernel(x_ref, gq_ref, gkv_ref, wq_ref, wk_ref, wvt_ref, wf_ref, bf_ref, selq_ref, selk_ref, cq_ref,
                ck_ref, qa_ref, ka_ref, vt_ref, carry):
    tm, d = x_ref.shape

    @pl.when(pl.program_id(1) == 0)
    def _():
        carry[...] = jnp.zeros_like(carry)

    x = x_ref[...]
    xn = x * lax.rsqrt(jnp.mean(x * x, axis=-1, keepdims=True) + EPS)
    hq = (xn * gq_ref[...]).astype(BF16)
    hkv32 = xn * gkv_ref[...]
    hkv = hkv32.astype(BF16)

    f_logit = _dot_split(hkv32, wf_ref) + bf_ref[...]
    logsig = jnp.minimum(f_logit, 0.0) - jnp.log1p(jnp.exp(-jnp.abs(f_logit)))

    lane = lax.broadcasted_iota(jnp.int32, logsig.shape, 1)
    by_group = lambda p: jnp.where(lane < N_HEADS, p[0], jnp.where(lane < 2 * N_HEADS, p[1],
                                                                   jnp.where(lane < 3 * N_HEADS, p[2], 0.0)))
    r_i = lax.broadcasted_iota(jnp.int32, (tm, tm), 0)
    c_i = lax.broadcasted_iota(jnp.int32, (tm, tm), 1)
    tri = (c_i <= r_i).astype(BF16)
    cum = _dot(tri, by_group(_split3(logsig)).astype(BF16))
    for shift in (N_HEADS, 2 * N_HEADS, 4 * N_HEADS):
        cum = cum + pltpu.roll(cum, shift, axis=1)
    dcum = cum + carry[0:1, :]
    carry[0:1, :] = dcum[tm - 1:tm, :]

    pieces = by_group(_split3(dcum * LOG2E)).astype(BF16)
    xq = _dot(pieces, selq_ref[...]) + cq_ref[...]
    xk = _dot(pieces, selk_ref[...]) + ck_ref[...]

    q = _dot(hq, wq_ref[...]) * (HEAD_DIM ** -0.5 * LOG2E)
    k = _dot(hkv, wk_ref[...])
    lane_a = lax.broadcasted_iota(jnp.int32, (tm, AUG), 1)
    for p in range(N_HEADS // 2):
        src = slice(p * AUG, (p + 1) * AUG)
        for odd in range(2):
            dst = slice((2 * p + odd) * AUG, (2 * p + odd + 1) * AUG)
            qv, kv = q[:, src], k[:, src]
            if odd:
                qv, kv = pltpu.roll(qv, HEAD_DIM, axis=1), pltpu.roll(kv, HEAD_DIM, axis=1)
            qa_ref[:, dst] = jnp.where(lane_a < HEAD_DIM, qv, xq[:, dst]).astype(BF16)
            ka_ref[:, dst] = jnp.where(lane_a < HEAD_DIM, kv, xk[:, dst]).astype(BF16)

    vt = _dot_nt(wvt_ref[...], hkv).astype(BF16)
    row = lax.broadcasted_iota(jnp.int32, (V_ROWS - HEAD_DIM, tm), 0)
    ones_row = (row == 0).astype(BF16)
    for h in range(N_HEADS):
        vt_ref[h, 0:HEAD_DIM, :] = vt[h * HEAD_DIM:(h + 1) * HEAD_DIM, :]
        vt_ref[h, HEAD_DIM:V_ROWS, :] = ones_row


def _qkv(x, g_q, g_kv, w_q, w_kvf, b_f, tm=512):
    bsz, seq, d = x.shape
    nh = N_HEADS
    w_k = w_kvf[:, :d]
    w_vt = w_kvf[:, d:2 * d].T
    w_f = w_kvf[:, 2 * d:]
    assert 8 * nh == LANES
    w_f3 = _hi_lo(jnp.tile(w_f, (1, 3)))
    b_f3 = jnp.zeros((1, LANES), F32).at[0, :3 * nh].set(jnp.tile(b_f.astype(F32), 3))

    j = jnp.arange(3)[:, None]
    hh = jnp.arange(nh)[None, :]
    src = (j * nh + hh).reshape(-1)
    dq = (hh * AUG + HEAD_DIM + j).reshape(-1)
    dk = (hh * AUG + HEAD_DIM + 3 + j).reshape(-1)
    selq = jnp.zeros((LANES, nh * AUG), BF16).at[src, dq].set(1)
    selk = jnp.zeros((LANES, nh * AUG), BF16).at[src, dk].set(1)
    cq = jnp.zeros((1, nh * AUG), F32).at[0, dk].set(-1.0)
    ck = jnp.zeros((1, nh * AUG), F32).at[0, dq].set(1.0)

    const = lambda shape: pl.BlockSpec(shape, lambda b, s: (0,) * len(shape))
    return pl.pallas_call(
        _qkv_kernel,
        grid=(bsz, seq // tm),
        in_specs=[
            pl.BlockSpec((None, tm, d), lambda b, s: (b, s, 0)),
            const((1, d)), const((1, d)), const((d, d)), const((d, d)), const((d, d)), const((d, 2 * LANES)),
            const((1, LANES)), const((LANES, nh * AUG)), const((LANES, nh * AUG)), const((1, nh * AUG)),
            const((1, nh * AUG)),
        ],
        out_specs=[
            pl.BlockSpec((None, tm, nh * AUG), lambda b, s: (b, s, 0)),
            pl.BlockSpec((None, tm, nh * AUG), lambda b, s: (b, s, 0)),
            pl.BlockSpec((None, nh, V_ROWS, tm), lambda b, s: (b, 0, 0, s)),
        ],
        out_shape=[
            jax.ShapeDtypeStruct((bsz, seq, nh * AUG), BF16),
            jax.ShapeDtypeStruct((bsz, seq, nh * AUG), BF16),
            jax.ShapeDtypeStruct((bsz, nh, V_ROWS, seq), BF16),
        ],
        scratch_shapes=[pltpu.VMEM((SUBLANES, LANES), F32)],
        compiler_params=_params(("parallel", "arbitrary"), 56),
        name="qkv",
    )(x, g_q.reshape(1, d).astype(F32), g_kv.reshape(1, d).astype(F32), w_q.astype(BF16), w_k.astype(BF16),
      w_vt.astype(BF16), w_f3, b_f3, selq, selk, cq, ck)


def _attn_kernel(qa_ref, ka_ref, vt_ref, o_ref, s_even, s_odd, acc_ref):
    tq = qa_ref.shape[0]
    tk = s_even.shape[1]
    qi = pl.program_id(2)
    heads = range(2)
    qs = [qa_ref[:, hl * AUG:(hl + 1) * AUG] for hl in heads]

    def rows(blk):
        return pl.ds(pl.multiple_of(blk * tk, tk), tk)

    def qk(blk, s_ref):
        for hl in heads:
            s_ref[hl] = _dot_nt(ka_ref[rows(blk), hl * AUG:(hl + 1) * AUG], qs[hl])

    def consume(blk, s_ref, ms, first_key=None):
        out = []
        for hl in heads:
            s = s_ref[hl]
            if first_key is not None:
                key = first_key + lax.broadcasted_iota(jnp.int32, s.shape, 0)
                qry = lax.broadcasted_iota(jnp.int32, s.shape, 1)
                s = jnp.where(key <= qry, s, -jnp.inf)
            m_new = jnp.maximum(ms[hl], jnp.max(s, axis=0, keepdims=True))
            p = jnp.exp2(s - m_new).astype(BF16)
            acc_ref[hl] = jnp.exp2(ms[hl] - m_new) * acc_ref[hl] + _dot(vt_ref[hl, :, rows(blk)], p)
            out.append(m_new)
        return tuple(out)

    acc_ref[...] = jnp.zeros_like(acc_ref)
    m_init = jnp.full((1, tq), -jnp.inf, F32)

    qk(0, s_even)

    def body(j, ms):
        qk(2 * j + 1, s_odd)
        ms = consume(2 * j, s_even, ms)
        qk(2 * j + 2, s_even)
        return consume(2 * j + 1, s_odd, ms)

    ms = lax.fori_loop(0, qi, body, (m_init, m_init))
    qk(2 * qi + 1, s_odd)
    ms = consume(2 * qi, s_even, ms, first_key=0)
    consume(2 * qi + 1, s_odd, ms, first_key=tk)

    outs = [acc_ref[hl, 0:HEAD_DIM, :] / acc_ref[hl, HEAD_DIM:HEAD_DIM + 1, :] for hl in heads]
    o_ref[...] = jnp.concatenate(outs, axis=0).T.astype(o_ref.dtype)


def _attention(qa, ka, vt, tq=512):
    bsz, seq, _ = qa.shape
    nh = N_HEADS
    tk = tq // 2
    return pl.pallas_call(
        _attn_kernel,
        grid=(bsz, nh // 2, seq // tq),
        in_specs=[
            pl.BlockSpec((None, tq, 2 * AUG), lambda b, h, i: (b, i, h)),
            pl.BlockSpec((None, seq, 2 * AUG), lambda b, h, i: (b, 0, h)),
            pl.BlockSpec((None, 2, V_ROWS, seq), lambda b, h, i: (b, h, 0, 0)),
        ],
        out_specs=pl.BlockSpec((None, tq, 2 * HEAD_DIM), lambda b, h, i: (b, i, h)),
        out_shape=jax.ShapeDtypeStruct((bsz, seq, nh * HEAD_DIM), BF16),
        scratch_shapes=[pltpu.VMEM((2, tk, tq), F32), pltpu.VMEM((2, tk, tq), F32),
                        pltpu.VMEM((2, V_ROWS, tq), F32)],
        compiler_params=_params(("parallel", "parallel", "arbitrary"), 48),
        name="attn",
    )(qa, ka, vt)


def _oproj_kernel(x_ref, o_ref, wo_ref, g_ref, wr_ref, br_ref, x3_ref, h_ref, gates_ref, route_ref, cnt_ref, count):
    tm = x_ref.shape[0]

    @pl.when(pl.program_id(0) == 0)
    def _():
        count[...] = jnp.zeros_like(count)

    x3 = x_ref[...] + _dot(o_ref[...], wo_ref[...])
    x3_ref[...] = x3
    h = _rms(x3, g_ref[...])
    h_ref[...] = h
    logits = _dot_split(h, wr_ref) + br_ref[...]
    lane = lax.broadcasted_iota(jnp.int32, logits.shape, 1)
    m1 = jnp.max(logits, axis=-1, keepdims=True)
    i1 = jnp.min(jnp.where(logits == m1, lane, LANES), axis=-1, keepdims=True)
    rest = jnp.where(lane == i1, -jnp.inf, logits)
    m2 = jnp.max(rest, axis=-1, keepdims=True)
    i2 = jnp.min(jnp.where(rest == m2, lane, LANES), axis=-1, keepdims=True)
    e = jnp.exp(m2 - m1)
    den = 1.0 + e
    gates_ref[...] = jnp.where(lane == 0, 1.0 / den, jnp.where(lane == 1, e / den, 0.0))

    hot = (lane == i1) | (lane == i2)
    r_i = lax.broadcasted_iota(jnp.int32, (tm, tm), 0)
    c_i = lax.broadcasted_iota(jnp.int32, (tm, tm), 1)
    rank = _dot((c_i < r_i).astype(BF16), hot.astype(BF16)) + count[0:1, :]
    count[0:1, :] = rank[tm - 1:tm, :] + hot[tm - 1:tm, :].astype(F32)
    cnt_ref[...] = count[...]
    rank0 = jnp.sum(jnp.where(lane == i1, rank, 0.0), axis=-1, keepdims=True)
    rank1 = jnp.sum(jnp.where(lane == i2, rank, 0.0), axis=-1, keepdims=True)
    packed = jnp.where(lane == 0, rank0, jnp.where(lane == 1, rank1, jnp.where(lane == 2, i1.astype(F32),
                                                                             jnp.where(lane == 3, i2.astype(F32), 0.0))))
    route_ref[...] = packed.T[0:SUBLANES, :].astype(jnp.int32)


def _oproj_router(x, o, w_o, g, router_w, router_b, tm):
    t, d = x.shape
    ne = router_w.shape[-1]
    br = jnp.full((1, LANES), NEG_BIG, F32).at[0, :ne].set(router_b.astype(F32))
    const = lambda shape: pl.BlockSpec(shape, lambda i: (0,) * len(shape))
    return pl.pallas_call(
        _oproj_kernel,
        grid=(t // tm,),
        in_specs=[pl.BlockSpec((tm, d), lambda i: (i, 0)), pl.BlockSpec((tm, d), lambda i: (i, 0)),
                  const((d, d)), const((1, d)), const((d, 2 * LANES)), const((1, LANES))],
        out_specs=[pl.BlockSpec((tm, d), lambda i: (i, 0)), pl.BlockSpec((tm, d), lambda i: (i, 0)),
                   pl.BlockSpec((tm, LANES), lambda i: (i, 0)),
                   pl.BlockSpec((None, SUBLANES, tm), lambda i: (i, 0, 0)),
                   const((SUBLANES, LANES))],
        out_shape=[jax.ShapeDtypeStruct((t, d), F32), jax.ShapeDtypeStruct((t, d), F32),
                   jax.ShapeDtypeStruct((t, LANES), F32),
                   jax.ShapeDtypeStruct((t // tm, SUBLANES, tm), jnp.int32),
                   jax.ShapeDtypeStruct((SUBLANES, LANES), F32)],
        scratch_shapes=[pltpu.VMEM((SUBLANES, LANES), F32)],
        compiler_params=_params(("arbitrary",), 40),
        name="oproj_router",
    )(x, o, w_o.astype(BF16), g.reshape(1, d).astype(F32), _hi_lo(router_w), br)


def _row_copies(src_of, dst_of, pos_ref, slot, sem, n_rows):
    def issue(r, carry):
        p = pos_ref[slot, r]
        pltpu.make_async_copy(src_of(r, p), dst_of(r, p), sem).start()
        return carry

    lax.fori_loop(0, n_rows, issue, 0, unroll=8)


def _zero_fill(zeros, xs_ref, sem, first_ref, last_ref, n_spans, tm):
    def pieces(i):
        a, b = first_ref[i], last_ref[i]
        a8 = jnp.minimum((a + SUBLANES - 1) // SUBLANES * SUBLANES, b)
        a_tm = jnp.minimum((a8 + tm - 1) // tm * tm, b)
        return a, a8, a_tm, b

    def each(act_row, act_8, act_tm):
        for i in range(n_spans):
            a, a8, a_tm, b = pieces(i)
            for r in range(SUBLANES - 1):
                @pl.when(a + r < a8)
                def _():
                    act_row(a + r)
            lax.fori_loop(0, (a_tm - a8) // SUBLANES,
                          lambda j, c: act_8(pl.multiple_of(a8 + j * SUBLANES, SUBLANES)) or c, 0)
            lax.fori_loop(0, (b - a_tm) // tm, lambda j, c: act_tm(pl.multiple_of(a_tm + j * tm, tm)) or c, 0)

    row = lambda p: pltpu.make_async_copy(zeros.at[pl.ds(0, 1)], xs_ref.at[pl.ds(p, 1)], sem)
    blk8 = lambda p: pltpu.make_async_copy(zeros.at[pl.ds(0, SUBLANES)], xs_ref.at[pl.ds(p, SUBLANES)], sem)
    blk = lambda p: pltpu.make_async_copy(zeros, xs_ref.at[pl.ds(p, tm)], sem)
    each(lambda p: row(p).start(), lambda p: blk8(p).start(), lambda p: blk(p).start())
    each(lambda p: row(p).wait(), lambda p: blk8(p).wait(), lambda p: blk(p).wait())


def _dispatch_kernel(n_spans, first_ref, last_ref, pos_ref, h_ref, xs_ref, zeros, sem):
    tm = h_ref.shape[0]
    for slot in range(TOP_K):
        _row_copies(lambda r, p: h_ref.at[pl.ds(r, 1)], lambda r, p: xs_ref.at[pl.ds(p, 1)], pos_ref, slot, sem, tm)
    for slot in range(TOP_K):
        pltpu.make_async_copy(h_ref, xs_ref.at[pl.ds(0, tm)], sem).wait()

    @pl.when(pl.program_id(0) == pl.num_programs(0) - 1)
    def _():
        zeros[...] = jnp.zeros_like(zeros)
        _zero_fill(zeros, xs_ref, sem, first_ref, last_ref, n_spans, tm)


def _dispatch(h, pos, span_first, span_last, n_rows, tm):
    t, d = h.shape
    grid_spec = pltpu.PrefetchScalarGridSpec(
        num_scalar_prefetch=2,
        grid=(t // tm,),
        in_specs=[pl.BlockSpec((None, SUBLANES, tm), lambda i, a, b: (i, 0, 0), memory_space=pltpu.SMEM),
                  pl.BlockSpec((tm, d), lambda i, a, b: (i, 0))],
        out_specs=pl.BlockSpec(memory_space=pl.ANY),
        scratch_shapes=[pltpu.VMEM((tm, d), F32), pltpu.SemaphoreType.DMA],
    )
    return pl.pallas_call(
        functools.partial(_dispatch_kernel, span_first.shape[0]),
        grid_spec=grid_spec,
        out_shape=jax.ShapeDtypeStruct((n_rows, d), F32),
        compiler_params=_params(("arbitrary",), 24),
        name="moe_dispatch",
    )(span_first, span_last, pos, h)


def _experts_kernel(te_ref, nt_ref, x_ref, wg_ref, wu_ref, wd_ref, y_ref):
    live = pl.program_id(0) < nt_ref[0]

    @pl.when(live)
    def _():
        y_ref[...] = _swiglu(x_ref[...].astype(BF16), wg_ref, wu_ref, wd_ref, 2)

    @pl.when(jnp.logical_not(live))
    def _():
        y_ref[...] = jnp.zeros_like(y_ref)


def _experts(xs, tile_expert, n_tiles, w_gate, w_up, w_down, tm):
    n_rows, d = xs.shape
    ne, _, f = w_gate.shape
    expert = lambda g, te, nt: (te[g], 0, 0)
    grid_spec = pltpu.PrefetchScalarGridSpec(
        num_scalar_prefetch=2,
        grid=(n_rows // tm,),
        in_specs=[pl.BlockSpec((tm, d), lambda g, te, nt: (g, 0)), pl.BlockSpec((None, d, f), expert),
                  pl.BlockSpec((None, d, f), expert), pl.BlockSpec((None, f, d), expert)],
        out_specs=pl.BlockSpec((tm, d), lambda g, te, nt: (g, 0)),
    )
    return pl.pallas_call(
        _experts_kernel,
        grid_spec=grid_spec,
        out_shape=jax.ShapeDtypeStruct((n_rows, d), F32),
        compiler_params=_params(("arbitrary",), 60),
        name="moe_experts",
    )(tile_expert, n_tiles, xs, w_gate.astype(BF16), w_up.astype(BF16), w_down.astype(BF16))


def _combine_kernel(pos_ref, x_ref, gates_ref, gf_ref, ys_ref, o_ref, ybuf, sem):
    tm = x_ref.shape[0]
    for slot in range(TOP_K):
        _row_copies(lambda r, p: ys_ref.at[pl.ds(p, 1)], lambda r, p, s=slot: ybuf.at[s, pl.ds(r, 1)], pos_ref, slot,
                    sem, tm)
    for slot in range(TOP_K):
        pltpu.make_async_copy(ys_ref.at[pl.ds(0, tm)], ybuf.at[slot], sem).wait()
    gates = gates_ref[...]
    mix = gates[:, 0:1] * ybuf[0] + gates[:, 1:2] * ybuf[1]
    o_ref[...] = _rms(x_ref[...] + mix, gf_ref[...])


def _combine(x, gates, pos, ys, g_final, tm):
    t, d = x.shape
    return pl.pallas_call(
        _combine_kernel,
        grid=(t // tm,),
        in_specs=[pl.BlockSpec((None, SUBLANES, tm), lambda i: (i, 0, 0), memory_space=pltpu.SMEM),
                  pl.BlockSpec((tm, d), lambda i: (i, 0)), pl.BlockSpec((tm, LANES), lambda i: (i, 0)),
                  pl.BlockSpec((1, d), lambda i: (0, 0)), pl.BlockSpec(memory_space=pl.ANY)],
        out_specs=pl.BlockSpec((tm, d), lambda i: (i, 0)),
        out_shape=jax.ShapeDtypeStruct((t, d), F32),
        scratch_shapes=[pltpu.VMEM((TOP_K, tm, d), F32), pltpu.SemaphoreType.DMA],
        compiler_params=_params(("arbitrary",), 32),
        name="moe_combine",
    )(pos, x, gates, g_final.reshape(1, d).astype(F32), ys)


def _moe(x3, h3, gates, route, counts, w_gate, w_up, w_down, g_final, tm):
    t, d = x3.shape
    ne = w_gate.shape[0]
    n_steps = TOP_K * t // tm + ne
    n = counts[0, :ne].astype(jnp.int32)
    tiles = (n + tm - 1) // tm
    first = jnp.cumsum(tiles) - tiles
    n_tiles = jnp.sum(tiles)
    live = jnp.minimum(jnp.arange(n_steps, dtype=jnp.int32), n_tiles - 1)
    tile_expert = (jnp.sum(live[:, None] >= first[None, :], axis=1) - 1).astype(jnp.int32)

    rank, eid = route[:, 0:TOP_K, :], route[:, TOP_K:2 * TOP_K, :]
    pos = jnp.pad(first[eid] * tm + rank, ((0, 0), (0, SUBLANES - TOP_K), (0, 0)))
    span_first = jnp.concatenate([first * tm + n, (n_tiles * tm)[None]]).astype(jnp.int32)
    span_last = jnp.concatenate([(first + tiles) * tm, jnp.full((1,), n_steps * tm)]).astype(jnp.int32)

    xs = _dispatch(h3, pos, span_first, span_last, n_steps * tm, tm)
    ys = _experts(xs, tile_expert, n_tiles.reshape(1).astype(jnp.int32), w_gate, w_up, w_down, tm)
    return _combine(x3, gates, pos, ys, g_final, tm)


def kernel(x, mix_norm, ffn_norm, conv_w_in, conv_b_in, conv_w_dw, conv_b_dw, conv_ln_g, conv_ln_b, conv_w_out,
           conv_b_out, kv_norm, w_kvf, b_f, w_q, w_o, ffn_w_gate, ffn_w_up, ffn_w_down, router_w, router_b,
           moe_w_gate, moe_w_up, moe_w_down, final_norm):
    bsz, seq, d = x.shape
    assert mix_norm.shape[0] == 2 and conv_w_in.shape[0] == 1 and w_q.shape[0] == 1 and moe_w_gate.shape[0] == 1
    assert d == N_HEADS * HEAD_DIM and conv_w_dw.shape[1] == CONV_WIDTH

    x1 = _mixer0(x, mix_norm[0], conv_w_in[0], conv_b_in[0], conv_w_dw[0], conv_b_dw[0], conv_ln_g[0],
                 conv_ln_b[0], conv_w_out[0], conv_b_out[0])
    x2 = _ffn0(x1.reshape(bsz * seq, d), ffn_norm[0], ffn_w_gate[0], ffn_w_up[0], ffn_w_down[0])
    qa, ka, vt = _qkv(x2.reshape(bsz, seq, d), mix_norm[1], kv_norm, w_q[0], w_kvf, b_f)
    o = _attention(qa, ka, vt)
    x3, h3, gates, route, counts = _oproj_router(x2, o.reshape(bsz * seq, d), w_o[0], ffn_norm[1], router_w[0],
                                                 router_b[0], MOE_ROWS)
    out = _moe(x3, h3, gates, route, counts, moe_w_gate[0], moe_w_up[0], moe_w_down[0], final_norm, MOE_ROWS)
    return out.reshape(bsz, seq, d)
```

```python
import functools

import jax
import jax.numpy as jnp
from jax import lax
from jax.experimental import pallas as pl
from jax.experimental.pallas import tpu as pltpu

F32 = jnp.float32
BF16 = jnp.bfloat16

EPS = 1e-6
N_HEADS = 16
HEAD_DIM = 64
CONV_WIDTH = 31
TOP_K = 2

LANES = 128
SUBLANES = 8
MIB = 1024 * 1024

HALO = 32
CONV_ROWS = 64
CONV_COLS = 256
AUG = 2 * HEAD_DIM
V_ROWS = HEAD_DIM + 16
MOE_ROWS = 512
NEG_BIG = -1e30
LOG2E = 1.4426950408889634


def _rms(x, g):
    return x * lax.rsqrt(jnp.mean(x * x, axis=-1, keepdims=True) + EPS) * g


def _dot(a, b):
    return jnp.dot(a, b, preferred_element_type=F32)


def _dot_nt(a, b):
    return lax.dot_general(a, b, (((1,), (1,)), ((), ())), preferred_element_type=F32)


def _split3(x):
    p0 = x.astype(BF16).astype(F32)
    r = x - p0
    p1 = r.astype(BF16).astype(F32)
    return p0, p1, (r - p1).astype(BF16).astype(F32)


def _hi_lo(w):
    w = jnp.pad(w.astype(F32), ((0, 0), (0, LANES - w.shape[1])))
    hi = w.astype(BF16)
    return jnp.concatenate([hi, (w - hi.astype(F32)).astype(BF16)], axis=1)


def _dot_split(x, w_ref):
    hi = x.astype(BF16)
    lo = (x - hi.astype(F32)).astype(BF16)
    both = _dot(hi, w_ref[...])
    return both[:, :LANES] + both[:, LANES:] + _dot(lo, w_ref[:, :LANES])


def _store_token_major(ref, x):
    n = x.shape[0]
    assert x.shape[1] == SUBLANES * LANES
    for s in range(SUBLANES):
        ref[pl.ds(s, n, stride=SUBLANES), :] = x[:, s * LANES:(s + 1) * LANES]


def _load_token_major(ref, n):
    return jnp.concatenate([ref[pl.ds(s, n, stride=SUBLANES), :] for s in range(SUBLANES)], axis=1)


def _token(ref, r, n=1):
    return ref.at[pl.ds(pl.multiple_of(r * SUBLANES, SUBLANES), n * SUBLANES)]


def _params(sem, vmem_mib):
    return pltpu.CompilerParams(dimension_semantics=sem, vmem_limit_bytes=vmem_mib * MIB)


def _mixer_kernel(x_ref, g_ref, win_ref, bin_ref, wdw_ref, bdw_ref, lng_ref, lnb_ref, wout_ref, bout_ref,
                  o_ref, ubuf, shifted, cbuf):
    tm, d = x_ref.shape

    @pl.when(pl.program_id(1) == 0)
    def _():
        ubuf[0:HALO, :] = jnp.zeros((HALO, d), F32)

    x = x_ref[...]
    h = _rms(x, g_ref[...]).astype(BF16)
    u = _dot(h, win_ref[...]) + bin_ref[...]
    ubuf[HALO:HALO + tm, :] = u[:, :d] * jax.nn.sigmoid(u[:, d:])

    n_sh = tm + HALO - SUBLANES
    for b in range(1, SUBLANES):
        shifted[b - 1, 0:n_sh, :] = ubuf[b:b + n_sh, :]

    first = HALO - (CONV_WIDTH - 1)

    def conv_rows(i, carry):
        r0 = pl.multiple_of(i * CONV_ROWS, CONV_ROWS)
        for c in range(d // CONV_COLS):
            cols = slice(c * CONV_COLS, (c + 1) * CONV_COLS)
            acc = jnp.zeros((CONV_ROWS, CONV_COLS), F32)
            for k in range(CONV_WIDTH):
                off = first + k
                a8, b = (off // SUBLANES) * SUBLANES, off % SUBLANES
                if b == 0:
                    win = ubuf[pl.ds(r0 + a8, CONV_ROWS), cols]
                else:
                    win = shifted[b - 1, pl.ds(r0 + a8, CONV_ROWS), cols]
                acc = acc + win * wdw_ref[k:k + 1, cols]
            cbuf[pl.ds(r0, CONV_ROWS), cols] = acc
        return carry

    lax.fori_loop(0, tm // CONV_ROWS, conv_rows, 0)

    ubuf[0:HALO, :] = ubuf[tm:tm + HALO, :]

    c = cbuf[...] + bdw_ref[...]
    mu = jnp.mean(c, axis=-1, keepdims=True)
    cc = c - mu
    var = jnp.mean(cc * cc, axis=-1, keepdims=True)
    y = cc * lax.rsqrt(var + EPS) * lng_ref[...] + lnb_ref[...]
    y = (y * jax.nn.sigmoid(y)).astype(BF16)
    o_ref[...] = x + _dot(y, wout_ref[...]) + bout_ref[...]


def _mixer0(x, g, w_in, b_in, w_dw, b_dw, ln_g, ln_b, w_out, b_out, tm=512):
    bsz, seq, d = x.shape
    row = lambda a: a.reshape(1, -1).astype(F32)
    const = lambda shape: pl.BlockSpec(shape, lambda b, s: (0,) * len(shape))
    return pl.pallas_call(
        _mixer_kernel,
        grid=(bsz, seq // tm),
        in_specs=[
            pl.BlockSpec((None, tm, d), lambda b, s: (b, s, 0)),
            const((1, d)), const((d, 2 * d)), const((1, 2 * d)), const((CONV_WIDTH, d)), const((1, d)),
            const((1, d)), const((1, d)), const((d, d)), const((1, d)),
        ],
        out_specs=pl.BlockSpec((None, tm, d), lambda b, s: (b, s, 0)),
        out_shape=jax.ShapeDtypeStruct((bsz, seq, d), F32),
        scratch_shapes=[
            pltpu.VMEM((tm + HALO, d), F32),
            pltpu.VMEM((SUBLANES - 1, tm + HALO - SUBLANES, d), F32),
            pltpu.VMEM((tm, d), F32),
        ],
        compiler_params=_params(("parallel", "arbitrary"), 56),
        name="mixer0",
    )(x, row(g), w_in.astype(BF16), row(b_in), w_dw.astype(F32), row(b_dw), row(ln_g), row(ln_b),
      w_out.astype(BF16), row(b_out))


def _swiglu(h, wg_ref, wu_ref, wd_ref, n_chunks):
    f = wg_ref.shape[-1]
    fc = f // n_chunks
    y = None
    for c in range(n_chunks):
        cols = slice(c * fc, (c + 1) * fc)
        g = _dot(h, wg_ref[:, cols])
        u = _dot(h, wu_ref[:, cols])
        a = (g * jax.nn.sigmoid(g) * u).astype(BF16)
        part = _dot(a, wd_ref[cols, :])
        y = part if y is None else y + part
    return y


def _ffn_kernel(x_ref, g_ref, wg_ref, wu_ref, wd_ref, o_ref):
    x = x_ref[...]
    h = _rms(x, g_ref[...]).astype(BF16)
    o_ref[...] = x + _swiglu(h, wg_ref, wu_ref, wd_ref, 2)


def _ffn0(x, g, w_gate, w_up, w_down, tm=512):
    t, d = x.shape
    f = w_gate.shape[-1]
    const = lambda shape: pl.BlockSpec(shape, lambda i: (0,) * len(shape), pipeline_mode=pl.Buffered(1))
    return pl.pallas_call(
        _ffn_kernel,
        grid=(t // tm,),
        in_specs=[pl.BlockSpec((tm, d), lambda i: (i, 0)), const((1, d)), const((d, f)), const((d, f)),
                  const((f, d))],
        out_specs=pl.BlockSpec((tm, d), lambda i: (i, 0)),
        out_shape=jax.ShapeDtypeStruct((t, d), F32),
        compiler_params=_params(("parallel",), 56),
        name="ffn0",
    )(x, g.reshape(1, d).astype(F32), w_gate.astype(BF16), w_up.astype(BF16), w_down.astype(BF16))


def _qkv_kernel(x_ref, gq_ref, gkv_ref, wq_ref, wk_ref, wvt_ref, wf_ref, bf_ref, selq_ref, selk_ref, cq_ref,
                ck_ref, qa_ref, ka_ref, vt_ref, carry):
    tm, d = x_ref.shape

    @pl.when(pl.program_id(1) == 0)
    def _():
        carry[...] = jnp.zeros_like(carry)

    x = x_ref[...]
    xn = x * lax.rsqrt(jnp.mean(x * x, axis=-1, keepdims=True) + EPS)
    hq = (xn * gq_ref[...]).astype(BF16)
    hkv32 = xn * gkv_ref[...]
    hkv = hkv32.astype(BF16)

    f_logit = _dot_split(hkv32, wf_ref) + bf_ref[...]
    logsig = jnp.minimum(f_logit, 0.0) - jnp.log1p(jnp.exp(-jnp.abs(f_logit)))

    lane = lax.broadcasted_iota(jnp.int32, logsig.shape, 1)
    by_group = lambda p: jnp.where(lane < N_HEADS, p[0], jnp.where(lane < 2 * N_HEADS, p[1],
                                                                   jnp.where(lane < 3 * N_HEADS, p[2], 0.0)))
    r_i = lax.broadcasted_iota(jnp.int32, (tm, tm), 0)
    c_i = lax.broadcasted_iota(jnp.int32, (tm, tm), 1)
    tri = (c_i <= r_i).astype(BF16)
    cum = _dot(tri, by_group(_split3(logsig)).astype(BF16))
    for shift in (N_HEADS, 2 * N_HEADS, 4 * N_HEADS):
        cum = cum + pltpu.roll(cum, shift, axis=1)
    dcum = cum + carry[0:1, :]
    carry[0:1, :] = dcum[tm - 1:tm, :]

    pieces = by_group(_split3(dcum * LOG2E)).astype(BF16)
    xq = _dot(pieces, selq_ref[...]) + cq_ref[...]
    xk = _dot(pieces, selk_ref[...]) + ck_ref[...]

    q = _dot(hq, wq_ref[...]) * (HEAD_DIM ** -0.5 * LOG2E)
    k = _dot(hkv, wk_ref[...])
    lane_a = lax.broadcasted_iota(jnp.int32, (tm, AUG), 1)
    for p in range(N_HEADS // 2):
        src = slice(p * AUG, (p + 1) * AUG)
        for odd in range(2):
            dst = slice((2 * p + odd) * AUG, (2 * p + odd + 1) * AUG)
            qv, kv = q[:, src], k[:, src]
            if odd:
                qv, kv = pltpu.roll(qv, HEAD_DIM, axis=1), pltpu.roll(kv, HEAD_DIM, axis=1)
            qa_ref[:, dst] = jnp.where(lane_a < HEAD_DIM, qv, xq[:, dst]).astype(BF16)
            ka_ref[:, dst] = jnp.where(lane_a < HEAD_DIM, kv, xk[:, dst]).astype(BF16)

    vt = _dot_nt(wvt_ref[...], hkv).astype(BF16)
    row = lax.broadcasted_iota(jnp.int32, (V_ROWS - HEAD_DIM, tm), 0)
    ones_row = (row == 0).astype(BF16)
    for h in range(N_HEADS):
        vt_ref[h, 0:HEAD_DIM, :] = vt[h * HEAD_DIM:(h + 1) * HEAD_DIM, :]
        vt_ref[h, HEAD_DIM:V_ROWS, :] = ones_row


def _qkv(x, g_q, g_kv, w_q, w_kvf, b_f, tm=512):
    bsz, seq, d = x.shape
    nh = N_HEADS
    w_k = w_kvf[:, :d]
    w_vt = w_kvf[:, d:2 * d].T
    w_f = w_kvf[:, 2 * d:]
    assert 8 * nh == LANES
    w_f3 = _hi_lo(jnp.tile(w_f, (1, 3)))
    b_f3 = jnp.zeros((1, LANES), F32).at[0, :3 * nh].set(jnp.tile(b_f.astype(F32), 3))

    j = jnp.arange(3)[:, None]
    hh = jnp.arange(nh)[None, :]
    src = (j * nh + hh).reshape(-1)
    dq = (hh * AUG + HEAD_DIM + j).reshape(-1)
    dk = (hh * AUG + HEAD_DIM + 3 + j).reshape(-1)
    selq = jnp.zeros((LANES, nh * AUG), BF16).at[src, dq].set(1)
    selk = jnp.zeros((LANES, nh * AUG), BF16).at[src, dk].set(1)
    cq = jnp.zeros((1, nh * AUG), F32).at[0, dk].set(-1.0)
    ck = jnp.zeros((1, nh * AUG), F32).at[0, dq].set(1.0)

    const = lambda shape: pl.BlockSpec(shape, lambda b, s: (0,) * len(shape))
    return pl.pallas_call(
        _qkv_kernel,
        grid=(bsz, seq // tm),
        in_specs=[
            pl.BlockSpec((None, tm, d), lambda b, s: (b, s, 0)),
            const((1, d)), const((1, d)), const((d, d)), const((d, d)), const((d, d)), const((d, 2 * LANES)),
            const((1, LANES)), const((LANES, nh * AUG)), const((LANES, nh * AUG)), const((1, nh * AUG)),
            const((1, nh * AUG)),
        ],
        out_specs=[
            pl.BlockSpec((None, tm, nh * AUG), lambda b, s: (b, s, 0)),
            pl.BlockSpec((None, tm, nh * AUG), lambda b, s: (b, s, 0)),
            pl.BlockSpec((None, nh, V_ROWS, tm), lambda b, s: (b, 0, 0, s)),
        ],
        out_shape=[
            jax.ShapeDtypeStruct((bsz, seq, nh * AUG), BF16),
            jax.ShapeDtypeStruct((bsz, seq, nh * AUG), BF16),
            jax.ShapeDtypeStruct((bsz, nh, V_ROWS, seq), BF16),
        ],
        scratch_shapes=[pltpu.VMEM((SUBLANES, LANES), F32)],
        compiler_params=_params(("parallel", "arbitrary"), 56),
        name="qkv",
    )(x, g_q.reshape(1, d).astype(F32), g_kv.reshape(1, d).astype(F32), w_q.astype(BF16), w_k.astype(BF16),
      w_vt.astype(BF16), w_f3, b_f3, selq, selk, cq, ck)


def _attn_kernel(qa_ref, ka_ref, vt_ref, o_ref, s_even, s_odd, acc_ref):
    tq = qa_ref.shape[0]
    tk = s_even.shape[1]
    qi = pl.program_id(2)
    heads = range(2)
    qs = [qa_ref[:, hl * AUG:(hl + 1) * AUG] for hl in heads]

    def rows(blk):
        return pl.ds(pl.multiple_of(blk * tk, tk), tk)

    def qk(blk, s_ref):
        for hl in heads:
            s_ref[hl] = _dot_nt(ka_ref[rows(blk), hl * AUG:(hl + 1) * AUG], qs[hl])

    def consume(blk, s_ref, ms, first_key=None):
        out = []
        for hl in heads:
            s = s_ref[hl]
            if first_key is not None:
                key = first_key + lax.broadcasted_iota(jnp.int32, s.shape, 0)
                qry = lax.broadcasted_iota(jnp.int32, s.shape, 1)
                s = jnp.where(key <= qry, s, -jnp.inf)
            m_new = jnp.maximum(ms[hl], jnp.max(s, axis=0, keepdims=True))
            p = jnp.exp2(s - m_new).astype(BF16)
            acc_ref[hl] = jnp.exp2(ms[hl] - m_new) * acc_ref[hl] + _dot(vt_ref[hl, :, rows(blk)], p)
            out.append(m_new)
        return tuple(out)

    def pair(j, ms):
        qk(2 * j + 1, s_odd)
        ms = consume(2 * j, s_even, ms)
        qk(2 * j + 2, s_even)
        return consume(2 * j + 1, s_odd, ms)

    acc_ref[...] = jnp.zeros_like(acc_ref)
    m_init = jnp.full((1, tq), -jnp.inf, F32)
    qk(0, s_even)
    ms = (m_init, m_init)
    for n in (4, 2, 1):
        trip = lambda i, ms, n=n: functools.reduce(lambda m, k: pair(n * i + k, m), range(n), ms)
        ms = lax.fori_loop(qi // (2 * n) * 2 if n < 4 else 0, qi // n, trip, ms)
    qk(2 * qi + 1, s_odd)
    ms = consume(2 * qi, s_even, ms, first_key=0)
    consume(2 * qi + 1, s_odd, ms, first_key=tk)

    outs = [acc_ref[hl, 0:HEAD_DIM, :] / acc_ref[hl, HEAD_DIM:HEAD_DIM + 1, :] for hl in heads]
    o_ref[...] = jnp.concatenate(outs, axis=0).T.astype(o_ref.dtype)


def _attention(qa, ka, vt, tq=512):
    bsz, seq, _ = qa.shape
    nh = N_HEADS
    tk = tq // 2
    return pl.pallas_call(
        _attn_kernel,
        grid=(bsz, nh // 2, seq // tq),
        in_specs=[
            pl.BlockSpec((None, tq, 2 * AUG), lambda b, h, i: (b, i, h)),
            pl.BlockSpec((None, seq, 2 * AUG), lambda b, h, i: (b, 0, h)),
            pl.BlockSpec((None, 2, V_ROWS, seq), lambda b, h, i: (b, h, 0, 0)),
        ],
        out_specs=pl.BlockSpec((None, tq, 2 * HEAD_DIM), lambda b, h, i: (b, i, h)),
        out_shape=jax.ShapeDtypeStruct((bsz, seq, nh * HEAD_DIM), BF16),
        scratch_shapes=[pltpu.VMEM((2, tk, tq), F32), pltpu.VMEM((2, tk, tq), F32),
                        pltpu.VMEM((2, V_ROWS, tq), F32)],
        compiler_params=_params(("parallel", "parallel", "arbitrary"), 48),
        name="attn",
    )(qa, ka, vt)


def _oproj_kernel(x_ref, o_ref, wo_ref, g_ref, wr_ref, br_ref, x3_ref, h_ref, gates_ref, route_ref, cnt_ref, count):
    tm = x_ref.shape[0]

    @pl.when(pl.program_id(0) == 0)
    def _():
        count[...] = jnp.zeros_like(count)

    x3 = x_ref[...] + _dot(o_ref[...], wo_ref[...])
    x3_ref[...] = x3
    h = _rms(x3, g_ref[...])
    _store_token_major(h_ref, h)
    logits = _dot_split(h, wr_ref) + br_ref[...]
    lane = lax.broadcasted_iota(jnp.int32, logits.shape, 1)
    m1 = jnp.max(logits, axis=-1, keepdims=True)
    i1 = jnp.min(jnp.where(logits == m1, lane, LANES), axis=-1, keepdims=True)
    rest = jnp.where(lane == i1, -jnp.inf, logits)
    m2 = jnp.max(rest, axis=-1, keepdims=True)
    i2 = jnp.min(jnp.where(rest == m2, lane, LANES), axis=-1, keepdims=True)
    e = jnp.exp(m2 - m1)
    den = 1.0 + e
    gates_ref[...] = jnp.where(lane == 0, 1.0 / den, jnp.where(lane == 1, e / den, 0.0))

    hot = (lane == i1) | (lane == i2)
    r_i = lax.broadcasted_iota(jnp.int32, (tm, tm), 0)
    c_i = lax.broadcasted_iota(jnp.int32, (tm, tm), 1)
    rank = _dot((c_i < r_i).astype(BF16), hot.astype(BF16)) + count[0:1, :]
    count[0:1, :] = rank[tm - 1:tm, :] + hot[tm - 1:tm, :].astype(F32)
    cnt_ref[...] = count[...]
    rank0 = jnp.sum(jnp.where(lane == i1, rank, 0.0), axis=-1, keepdims=True)
    rank1 = jnp.sum(jnp.where(lane == i2, rank, 0.0), axis=-1, keepdims=True)
    packed = jnp.where(lane == 0, rank0, jnp.where(lane == 1, rank1, jnp.where(lane == 2, i1.astype(F32),
                                                                             jnp.where(lane == 3, i2.astype(F32), 0.0))))
    route_ref[...] = packed.T[0:SUBLANES, :].astype(jnp.int32)


def _oproj_router(x, o, w_o, g, router_w, router_b, tm):
    t, d = x.shape
    ne = router_w.shape[-1]
    br = jnp.full((1, LANES), NEG_BIG, F32).at[0, :ne].set(router_b.astype(F32))
    const = lambda shape: pl.BlockSpec(shape, lambda i: (0,) * len(shape))
    return pl.pallas_call(
        _oproj_kernel,
        grid=(t // tm,),
        in_specs=[pl.BlockSpec((tm, d), lambda i: (i, 0)), pl.BlockSpec((tm, d), lambda i: (i, 0)),
                  const((d, d)), const((1, d)), const((d, 2 * LANES)), const((1, LANES))],
        out_specs=[pl.BlockSpec((tm, d), lambda i: (i, 0)), pl.BlockSpec((tm * SUBLANES, LANES), lambda i: (i, 0)),
                   pl.BlockSpec((tm, LANES), lambda i: (i, 0)),
                   pl.BlockSpec((None, SUBLANES, tm), lambda i: (i, 0, 0)),
                   const((SUBLANES, LANES))],
        out_shape=[jax.ShapeDtypeStruct((t, d), F32), jax.ShapeDtypeStruct((t * SUBLANES, LANES), F32),
                   jax.ShapeDtypeStruct((t, LANES), F32),
                   jax.ShapeDtypeStruct((t // tm, SUBLANES, tm), jnp.int32),
                   jax.ShapeDtypeStruct((SUBLANES, LANES), F32)],
        scratch_shapes=[pltpu.VMEM((SUBLANES, LANES), F32)],
        compiler_params=_params(("arbitrary",), 40),
        name="oproj_router",
    )(x, o, w_o.astype(BF16), g.reshape(1, d).astype(F32), _hi_lo(router_w), br)


def _token_copies(copy_of, pos_ref, sem, n_tokens):
    def issue(r, carry):
        for slot in range(TOP_K):
            pltpu.make_async_copy(*copy_of(r, pos_ref[slot, r], slot), sem).start()
        return carry

    lax.fori_loop(0, n_tokens, issue, 0, unroll=8)


def _zero_fill(zeros, xs_ref, sem, first_ref, last_ref, n_spans, tm):
    bits = [1 << k for k in reversed(range(tm.bit_length() - 1))]

    def each(act):
        for i in range(n_spans):
            a = first_ref[i]
            length = last_ref[i] - a
            lax.fori_loop(0, length // tm, lambda j, c: act(a + j * tm, tm) or c, 0)
            done = length // tm * tm
            for nb in bits:
                @pl.when(length & nb != 0)
                def _():
                    act(a + done + (length % tm) // (2 * nb) * (2 * nb), nb)

    copy = lambda p, n: pltpu.make_async_copy(_token(zeros, 0, n), _token(xs_ref, p, n), sem)
    each(lambda p, n: copy(p, n).start())
    each(lambda p, n: copy(p, n).wait())


def _dispatch_kernel(n_spans, first_ref, last_ref, pos_ref, h_ref, xs_ref, zeros, sem):
    tm = h_ref.shape[0] // SUBLANES
    _token_copies(lambda r, p, slot: (_token(h_ref, r), _token(xs_ref, p)), pos_ref, sem, tm)
    for slot in range(TOP_K):
        pltpu.make_async_copy(h_ref, _token(xs_ref, 0, tm), sem).wait()

    @pl.when(pl.program_id(0) == pl.num_programs(0) - 1)
    def _():
        zeros[...] = jnp.zeros_like(zeros)
        _zero_fill(zeros, xs_ref, sem, first_ref, last_ref, n_spans, tm)


def _dispatch(h, pos, span_first, span_last, n_tokens, tm):
    assert tm & (tm - 1) == 0
    grid_spec = pltpu.PrefetchScalarGridSpec(
        num_scalar_prefetch=2,
        grid=(h.shape[0] // (tm * SUBLANES),),
        in_specs=[pl.BlockSpec((None, SUBLANES, tm), lambda i, a, b: (i, 0, 0), memory_space=pltpu.SMEM),
                  pl.BlockSpec((tm * SUBLANES, LANES), lambda i, a, b: (i, 0))],
        out_specs=pl.BlockSpec(memory_space=pl.ANY),
        scratch_shapes=[pltpu.VMEM((tm * SUBLANES, LANES), F32), pltpu.SemaphoreType.DMA],
    )
    return pl.pallas_call(
        functools.partial(_dispatch_kernel, span_first.shape[0]),
        grid_spec=grid_spec,
        out_shape=jax.ShapeDtypeStruct((n_tokens * SUBLANES, LANES), F32),
        compiler_params=_params(("arbitrary",), 24),
        name="moe_dispatch",
    )(span_first, span_last, pos, h)


def _experts_kernel(te_ref, nt_ref, x_ref, wg_ref, wu_ref, wd_ref, y_ref):
    tm = x_ref.shape[0] // SUBLANES
    live = pl.program_id(0) < nt_ref[0]

    @pl.when(live)
    def _():
        x = _load_token_major(x_ref, tm).astype(BF16)
        _store_token_major(y_ref, _swiglu(x, wg_ref, wu_ref, wd_ref, 2))

    @pl.when(jnp.logical_not(live))
    def _():
        y_ref[...] = jnp.zeros_like(y_ref)


def _experts(xs, tile_expert, n_tiles, w_gate, w_up, w_down, tm):
    ne, d, f = w_gate.shape
    expert = lambda g, te, nt: (te[g], 0, 0)
    rows = pl.BlockSpec((tm * SUBLANES, LANES), lambda g, te, nt: (g, 0))
    grid_spec = pltpu.PrefetchScalarGridSpec(
        num_scalar_prefetch=2,
        grid=(xs.shape[0] // (tm * SUBLANES),),
        in_specs=[rows, pl.BlockSpec((None, d, f), expert), pl.BlockSpec((None, d, f), expert),
                  pl.BlockSpec((None, f, d), expert)],
        out_specs=rows,
    )
    return pl.pallas_call(
        _experts_kernel,
        grid_spec=grid_spec,
        out_shape=jax.ShapeDtypeStruct(xs.shape, F32),
        compiler_params=_params(("arbitrary",), 60),
        name="moe_experts",
    )(tile_expert, n_tiles, xs, w_gate.astype(BF16), w_up.astype(BF16), w_down.astype(BF16))


def _combine_kernel(pos_ref, nxt_ref, x_ref, gates_ref, gf_ref, ys_ref, o_ref, ybuf, sems):
    tm = x_ref.shape[0]
    i = pl.program_id(0)
    half = lax.rem(i, 2)

    def fetch(p_ref, h):
        _token_copies(lambda r, p, slot: (_token(ys_ref, p), _token(ybuf.at[h, slot], r)), p_ref, sems.at[h], tm)

    @pl.when(i == 0)
    def _():
        fetch(pos_ref, 0)

    @pl.when(i + 1 < pl.num_programs(0))
    def _():
        fetch(nxt_ref, 1 - half)

    for slot in range(TOP_K):
        pltpu.make_async_copy(_token(ys_ref, 0, tm), ybuf.at[half, slot], sems.at[half]).wait()
    gates = gates_ref[...]
    mix = (gates[:, 0:1] * _load_token_major(ybuf.at[half, 0], tm)
           + gates[:, 1:2] * _load_token_major(ybuf.at[half, 1], tm))
    o_ref[...] = _rms(x_ref[...] + mix, gf_ref[...])


def _combine(x, gates, pos, ys, g_final, tm):
    t, d = x.shape
    n = t // tm
    routes = lambda imap: pl.BlockSpec((None, SUBLANES, tm), imap, memory_space=pltpu.SMEM)
    return pl.pallas_call(
        _combine_kernel,
        grid=(n,),
        in_specs=[routes(lambda i: (i, 0, 0)), routes(lambda i: (jnp.minimum(i + 1, n - 1), 0, 0)),
                  pl.BlockSpec((tm, d), lambda i: (i, 0)), pl.BlockSpec((tm, LANES), lambda i: (i, 0)),
                  pl.BlockSpec((1, d), lambda i: (0, 0)), pl.BlockSpec(memory_space=pl.ANY)],
        out_specs=pl.BlockSpec((tm, d), lambda i: (i, 0)),
        out_shape=jax.ShapeDtypeStruct((t, d), F32),
        scratch_shapes=[pltpu.VMEM((2, TOP_K, tm * SUBLANES, LANES), F32), pltpu.SemaphoreType.DMA((2,))],
        compiler_params=_params(("arbitrary",), 40),
        name="moe_combine",
    )(pos, pos, x, gates, g_final.reshape(1, d).astype(F32), ys)


def _moe(x3, h3, gates, route, counts, w_gate, w_up, w_down, g_final, tm):
    t, d = x3.shape
    ne = w_gate.shape[0]
    n_steps = TOP_K * t // tm + ne
    n = counts[0, :ne].astype(jnp.int32)
    tiles = (n + tm - 1) // tm
    first = jnp.cumsum(tiles) - tiles
    n_tiles = jnp.sum(tiles)
    live = jnp.minimum(jnp.arange(n_steps, dtype=jnp.int32), n_tiles - 1)
    tile_expert = (jnp.sum(live[:, None] >= first[None, :], axis=1) - 1).astype(jnp.int32)

    rank, eid = route[:, 0:TOP_K, :], route[:, TOP_K:2 * TOP_K, :]
    pos = jnp.pad(first[eid] * tm + rank, ((0, 0), (0, SUBLANES - TOP_K), (0, 0)))
    span_first = jnp.concatenate([first * tm + n, (n_tiles * tm)[None]]).astype(jnp.int32)
    span_last = jnp.concatenate([(first + tiles) * tm, jnp.full((1,), n_steps * tm)]).astype(jnp.int32)

    xs = _dispatch(h3, pos, span_first, span_last, n_steps * tm, tm)
    ys = _experts(xs, tile_expert, n_tiles.reshape(1).astype(jnp.int32), w_gate, w_up, w_down, tm)
    return _combine(x3, gates, pos, ys, g_final, tm)


def kernel(x, mix_norm, ffn_norm, conv_w_in, conv_b_in, conv_w_dw, conv_b_dw, conv_ln_g, conv_ln_b, conv_w_out,
           conv_b_out, kv_norm, w_kvf, b_f, w_q, w_o, ffn_w_gate, ffn_w_up, ffn_w_down, router_w, router_b,
           moe_w_gate, moe_w_up, moe_w_down, final_norm):
    bsz, seq, d = x.shape
    assert mix_norm.shape[0] == 2 and conv_w_in.shape[0] == 1 and w_q.shape[0] == 1 and moe_w_gate.shape[0] == 1
    assert d == N_HEADS * HEAD_DIM and conv_w_dw.shape[1] == CONV_WIDTH

    x1 = _mixer0(x, mix_norm[0], conv_w_in[0], conv_b_in[0], conv_w_dw[0], conv_b_dw[0], conv_ln_g[0],
                 conv_ln_b[0], conv_w_out[0], conv_b_out[0])
    x2 = _ffn0(x1.reshape(bsz * seq, d), ffn_norm[0], ffn_w_gate[0], ffn_w_up[0], ffn_w_down[0])
    qa, ka, vt = _qkv(x2.reshape(bsz, seq, d), mix_norm[1], kv_norm, w_q[0], w_kvf, b_f)
    o = _attention(qa, ka, vt)
    x3, h3, gates, route, counts = _oproj_router(x2, o.reshape(bsz * seq, d), w_o[0], ffn_norm[1], router_w[0],
                                                 router_b[0], MOE_ROWS)
    out = _moe(x3, h3, gates, route, counts, moe_w_gate[0], moe_w_up[0], moe_w_down[0], final_norm, MOE_ROWS)
    return out.reshape(bsz, seq, d)
```

```python
import functools

import jax
import jax.numpy as jnp
from jax import lax
from jax.experimental import pallas as pl
from jax.experimental.pallas import tpu as pltpu

F32 = jnp.float32
BF16 = jnp.bfloat16

EPS = 1e-6
N_HEADS = 16
HEAD_DIM = 64
CONV_WIDTH = 31
TOP_K = 2

LANES = 128
SUBLANES = 8
MIB = 1024 * 1024

HALO = 32
CONV_ROWS = 64
CONV_COLS = 256
AUG = 2 * HEAD_DIM
V_ROWS = HEAD_DIM + 16
MOE_ROWS = 512
NEG_BIG = -1e30
LOG2E = 1.4426950408889634


def _rms(x, g):
    return x * lax.rsqrt(jnp.mean(x * x, axis=-1, keepdims=True) + EPS) * g


def _dot(a, b):
    return jnp.dot(a, b, preferred_element_type=F32)


def _dot_nt(a, b):
    return lax.dot_general(a, b, (((1,), (1,)), ((), ())), preferred_element_type=F32)


def _split3(x):
    p0 = x.astype(BF16).astype(F32)
    r = x - p0
    p1 = r.astype(BF16).astype(F32)
    return p0, p1, (r - p1).astype(BF16).astype(F32)


def _hi_lo(w):
    w = jnp.pad(w.astype(F32), ((0, 0), (0, LANES - w.shape[1])))
    hi = w.astype(BF16)
    return jnp.concatenate([hi, (w - hi.astype(F32)).astype(BF16)], axis=1)


def _dot_split(x, w_ref):
    hi = x.astype(BF16)
    lo = (x - hi.astype(F32)).astype(BF16)
    both = _dot(hi, w_ref[...])
    return both[:, :LANES] + both[:, LANES:] + _dot(lo, w_ref[:, :LANES])


def _store_token_major(ref, x):
    n = x.shape[0]
    assert x.shape[1] == SUBLANES * LANES
    for s in range(SUBLANES):
        ref[pl.ds(s, n, stride=SUBLANES), :] = x[:, s * LANES:(s + 1) * LANES]


def _load_token_major(ref, n):
    return jnp.concatenate([ref[pl.ds(s, n, stride=SUBLANES), :] for s in range(SUBLANES)], axis=1)


def _token(ref, r, n=1):
    return ref.at[pl.ds(pl.multiple_of(r * SUBLANES, SUBLANES), n * SUBLANES)]


def _params(sem, vmem_mib):
    return pltpu.CompilerParams(dimension_semantics=sem, vmem_limit_bytes=vmem_mib * MIB)


def _mixer_kernel(x_ref, g_ref, win_ref, bin_ref, wdw_ref, bdw_ref, lng_ref, lnb_ref, wout_ref, bout_ref,
                  o_ref, ubuf, shifted, cbuf):
    tm, d = x_ref.shape

    @pl.when(pl.program_id(1) == 0)
    def _():
        ubuf[0:HALO, :] = jnp.zeros((HALO, d), F32)

    x = x_ref[...]
    h = _rms(x, g_ref[...]).astype(BF16)
    u = _dot(h, win_ref[...]) + bin_ref[...]
    ubuf[HALO:HALO + tm, :] = u[:, :d] * jax.nn.sigmoid(u[:, d:])

    n_sh = tm + HALO - SUBLANES
    for b in range(1, SUBLANES):
        shifted[b - 1, 0:n_sh, :] = ubuf[b:b + n_sh, :]

    first = HALO - (CONV_WIDTH - 1)

    n_blk = CONV_ROWS // SUBLANES
    phases = [[(k, (first + k) // SUBLANES) for k in range(CONV_WIDTH) if (first + k) % SUBLANES == b]
              for b in range(SUBLANES)]

    def conv_rows(i, carry):
        r0 = pl.multiple_of(i * CONV_ROWS, CONV_ROWS)
        for c in range(d // CONV_COLS):
            cols = slice(c * CONV_COLS, (c + 1) * CONV_COLS)
            accs = [jnp.zeros((SUBLANES, CONV_COLS), F32) for _ in range(n_blk)]
            for b, taps in enumerate(phases):
                src = ubuf if b == 0 else shifted.at[b - 1]
                wts = {k: jnp.broadcast_to(wdw_ref[k:k + 1, cols], (SUBLANES, CONV_COLS)) for k, _ in taps}
                for j in range(min(a for _, a in taps), n_blk + max(a for _, a in taps)):
                    win = src[pl.ds(r0 + j * SUBLANES, SUBLANES), cols]
                    for k, a in taps:
                        if 0 <= j - a < n_blk:
                            accs[j - a] = accs[j - a] + win * wts[k]
            for t in range(n_blk):
                cbuf[pl.ds(r0 + t * SUBLANES, SUBLANES), cols] = accs[t]
        return carry

    lax.fori_loop(0, tm // CONV_ROWS, conv_rows, 0)

    ubuf[0:HALO, :] = ubuf[tm:tm + HALO, :]

    c = cbuf[...] + bdw_ref[...]
    mu = jnp.mean(c, axis=-1, keepdims=True)
    cc = c - mu
    var = jnp.mean(cc * cc, axis=-1, keepdims=True)
    y = cc * lax.rsqrt(var + EPS) * lng_ref[...] + lnb_ref[...]
    y = (y * jax.nn.sigmoid(y)).astype(BF16)
    o_ref[...] = x + _dot(y, wout_ref[...]) + bout_ref[...]


def _mixer0(x, g, w_in, b_in, w_dw, b_dw, ln_g, ln_b, w_out, b_out, tm=512):
    bsz, seq, d = x.shape
    row = lambda a: a.reshape(1, -1).astype(F32)
    const = lambda shape: pl.BlockSpec(shape, lambda b, s: (0,) * len(shape))
    return pl.pallas_call(
        _mixer_kernel,
        grid=(bsz, seq // tm),
        in_specs=[
            pl.BlockSpec((None, tm, d), lambda b, s: (b, s, 0)),
            const((1, d)), const((d, 2 * d)), const((1, 2 * d)), const((CONV_WIDTH, d)), const((1, d)),
            const((1, d)), const((1, d)), const((d, d)), const((1, d)),
        ],
        out_specs=pl.BlockSpec((None, tm, d), lambda b, s: (b, s, 0)),
        out_shape=jax.ShapeDtypeStruct((bsz, seq, d), F32),
        scratch_shapes=[
            pltpu.VMEM((tm + HALO, d), F32),
            pltpu.VMEM((SUBLANES - 1, tm + HALO - SUBLANES, d), F32),
            pltpu.VMEM((tm, d), F32),
        ],
        compiler_params=_params(("parallel", "arbitrary"), 56),
        name="mixer0",
    )(x, row(g), w_in.astype(BF16), row(b_in), w_dw.astype(F32), row(b_dw), row(ln_g), row(ln_b),
      w_out.astype(BF16), row(b_out))


def _swiglu(h, wg_ref, wu_ref, wd_ref, n_chunks):
    f = wg_ref.shape[-1]
    fc = f // n_chunks
    y = None
    for c in range(n_chunks):
        cols = slice(c * fc, (c + 1) * fc)
        g = _dot(h, wg_ref[:, cols])
        u = _dot(h, wu_ref[:, cols])
        a = (g * jax.nn.sigmoid(g) * u).astype(BF16)
        part = _dot(a, wd_ref[cols, :])
        y = part if y is None else y + part
    return y


def _ffn_kernel(x_ref, g_ref, wg_ref, wu_ref, wd_ref, o_ref):
    x = x_ref[...]
    h = _rms(x, g_ref[...]).astype(BF16)
    o_ref[...] = x + _swiglu(h, wg_ref, wu_ref, wd_ref, 1)


def _ffn0(x, g, w_gate, w_up, w_down, tm=512):
    t, d = x.shape
    f = w_gate.shape[-1]
    const = lambda shape: pl.BlockSpec(shape, lambda i: (0,) * len(shape), pipeline_mode=pl.Buffered(1))
    return pl.pallas_call(
        _ffn_kernel,
        grid=(t // tm,),
        in_specs=[pl.BlockSpec((tm, d), lambda i: (i, 0)), const((1, d)), const((d, f)), const((d, f)),
                  const((f, d))],
        out_specs=pl.BlockSpec((tm, d), lambda i: (i, 0)),
        out_shape=jax.ShapeDtypeStruct((t, d), F32),
        compiler_params=_params(("parallel",), 56),
        name="ffn0",
    )(x, g.reshape(1, d).astype(F32), w_gate.astype(BF16), w_up.astype(BF16), w_down.astype(BF16))


def _qkv_kernel(x_ref, gq_ref, gkv_ref, wq_ref, wk_ref, wvt_ref, wf_ref, bf_ref, selq_ref, selk_ref, cq_ref,
                ck_ref, qa_ref, ka_ref, vt_ref, carry):
    tm, d = x_ref.shape

    @pl.when(pl.program_id(1) == 0)
    def _():
        carry[...] = jnp.zeros_like(carry)

    x = x_ref[...]
    xn = x * lax.rsqrt(jnp.mean(x * x, axis=-1, keepdims=True) + EPS)
    hq = (xn * gq_ref[...]).astype(BF16)
    hkv32 = xn * gkv_ref[...]
    hkv = hkv32.astype(BF16)

    f_logit = _dot_split(hkv32, wf_ref) + bf_ref[...]
    logsig = jnp.minimum(f_logit, 0.0) - jnp.log1p(jnp.exp(-jnp.abs(f_logit)))

    lane = lax.broadcasted_iota(jnp.int32, logsig.shape, 1)
    by_group = lambda p: jnp.where(lane < N_HEADS, p[0], jnp.where(lane < 2 * N_HEADS, p[1],
                                                                   jnp.where(lane < 3 * N_HEADS, p[2], 0.0)))
    r_i = lax.broadcasted_iota(jnp.int32, (tm, tm), 0)
    c_i = lax.broadcasted_iota(jnp.int32, (tm, tm), 1)
    tri = (c_i <= r_i).astype(BF16)
    cum = _dot(tri, by_group(_split3(logsig)).astype(BF16))
    for shift in (N_HEADS, 2 * N_HEADS, 4 * N_HEADS):
        cum = cum + pltpu.roll(cum, shift, axis=1)
    dcum = cum + carry[0:1, :]
    carry[0:1, :] = dcum[tm - 1:tm, :]

    pieces = by_group(_split3(dcum * LOG2E)).astype(BF16)
    xq = _dot(pieces, selq_ref[...]) + cq_ref[...]
    xk = _dot(pieces, selk_ref[...]) + ck_ref[...]

    q = _dot(hq, wq_ref[...]) * (HEAD_DIM ** -0.5 * LOG2E)
    k = _dot(hkv, wk_ref[...])
    lane_a = lax.broadcasted_iota(jnp.int32, (tm, AUG), 1)
    for p in range(N_HEADS // 2):
        src = slice(p * AUG, (p + 1) * AUG)
        for odd in range(2):
            dst = slice((2 * p + odd) * AUG, (2 * p + odd + 1) * AUG)
            qv, kv = q[:, src], k[:, src]
            if odd:
                qv, kv = pltpu.roll(qv, HEAD_DIM, axis=1), pltpu.roll(kv, HEAD_DIM, axis=1)
            qa_ref[:, dst] = jnp.where(lane_a < HEAD_DIM, qv, xq[:, dst]).astype(BF16)
            ka_ref[:, dst] = jnp.where(lane_a < HEAD_DIM, kv, xk[:, dst]).astype(BF16)

    vt = _dot_nt(wvt_ref[...], hkv).astype(BF16)
    row = lax.broadcasted_iota(jnp.int32, (V_ROWS - HEAD_DIM, tm), 0)
    ones_row = (row == 0).astype(BF16)
    for h in range(N_HEADS):
        vt_ref[h, 0:HEAD_DIM, :] = vt[h * HEAD_DIM:(h + 1) * HEAD_DIM, :]
        vt_ref[h, HEAD_DIM:V_ROWS, :] = ones_row


def _qkv(x, g_q, g_kv, w_q, w_kvf, b_f, tm=512):
    bsz, seq, d = x.shape
    nh = N_HEADS
    w_k = w_kvf[:, :d]
    w_vt = w_kvf[:, d:2 * d].T
    w_f = w_kvf[:, 2 * d:]
    assert 8 * nh == LANES
    w_f3 = _hi_lo(jnp.tile(w_f, (1, 3)))
    b_f3 = jnp.zeros((1, LANES), F32).at[0, :3 * nh].set(jnp.tile(b_f.astype(F32), 3))

    j = jnp.arange(3)[:, None]
    hh = jnp.arange(nh)[None, :]
    col = lambda i: (hh * AUG + HEAD_DIM + i).reshape(-1)
    src = (j * nh + hh).reshape(-1)
    selq = jnp.zeros((LANES, nh * AUG), BF16).at[src, col(j)].set(1)
    selk = jnp.zeros((LANES, nh * AUG), BF16).at[src, col(3 + j)].set(1)
    cq = jnp.zeros((1, nh * AUG), F32).at[0, col(3 + j)].set(-1.0)
    ck = jnp.zeros((1, nh * AUG), F32).at[0, col(j)].set(1.0)

    const = lambda shape: pl.BlockSpec(shape, lambda b, s: (0,) * len(shape))
    return pl.pallas_call(
        _qkv_kernel,
        grid=(bsz, seq // tm),
        in_specs=[
            pl.BlockSpec((None, tm, d), lambda b, s: (b, s, 0)),
            const((1, d)), const((1, d)), const((d, d)), const((d, d)), const((d, d)), const((d, 2 * LANES)),
            const((1, LANES)), const((LANES, nh * AUG)), const((LANES, nh * AUG)), const((1, nh * AUG)),
            const((1, nh * AUG)),
        ],
        out_specs=[
            pl.BlockSpec((None, tm, nh * AUG), lambda b, s: (b, s, 0)),
            pl.BlockSpec((None, tm, nh * AUG), lambda b, s: (b, s, 0)),
            pl.BlockSpec((None, nh, V_ROWS, tm), lambda b, s: (b, 0, 0, s)),
        ],
        out_shape=[
            jax.ShapeDtypeStruct((bsz, seq, nh * AUG), BF16),
            jax.ShapeDtypeStruct((bsz, seq, nh * AUG), BF16),
            jax.ShapeDtypeStruct((bsz, nh, V_ROWS, seq), BF16),
        ],
        scratch_shapes=[pltpu.VMEM((SUBLANES, LANES), F32)],
        compiler_params=_params(("parallel", "arbitrary"), 56),
        name="qkv",
    )(x, g_q.reshape(1, d).astype(F32), g_kv.reshape(1, d).astype(F32), w_q.astype(BF16), w_k.astype(BF16),
      w_vt.astype(BF16), w_f3, b_f3, selq, selk, cq, ck)


def _pair_trips(n_pairs, pair, carry):
    for n in (4, 2, 1):
        trip = lambda i, c, n=n: functools.reduce(lambda c, k: pair(n * i + k, c), range(n), c)
        carry = lax.fori_loop(n_pairs // (2 * n) * 2 if n < 4 else 0, n_pairs // n, trip, carry)
    return carry


def _attn_kernel(qa_ref, ka_ref, vt_ref, o_ref, s_even, s_odd, acc_ref):
    tq = qa_ref.shape[0]
    tk = s_even.shape[1]
    qi = pl.program_id(2)
    heads = range(2)
    qs = [qa_ref[:, hl * AUG:(hl + 1) * AUG] for hl in heads]

    def rows(blk):
        return pl.ds(pl.multiple_of(blk * tk, tk), tk)

    def scores(blk, hl):
        return _dot_nt(ka_ref[rows(blk), hl * AUG:(hl + 1) * AUG], qs[hl])

    def causal(s, first_key):
        key = first_key + lax.broadcasted_iota(jnp.int32, s.shape, 0)
        qry = lax.broadcasted_iota(jnp.int32, s.shape, 1)
        return jnp.where(key <= qry, s, -jnp.inf)

    def qk(blk, s_ref):
        for hl in heads:
            s_ref[hl] = scores(blk, hl)

    def consume(blk, s_ref, ms, first_key=None):
        out = []
        for hl in heads:
            s = s_ref[hl] if first_key is None else causal(s_ref[hl], first_key)
            m_new = jnp.maximum(ms[hl], jnp.max(s, axis=0, keepdims=True))
            p = jnp.exp2(s - m_new).astype(BF16)
            acc_ref[hl] = jnp.exp2(ms[hl] - m_new) * acc_ref[hl] + _dot(vt_ref[hl, :, rows(blk)], p)
            out.append(m_new)
        return tuple(out)

    def pair(j, ms):
        qk(2 * j + 1, s_odd)
        ms = consume(2 * j, s_even, ms)
        qk(2 * j + 2, s_even)
        return consume(2 * j + 1, s_odd, ms)

    acc_ref[...] = jnp.zeros_like(acc_ref)
    m_init = jnp.full((1, tq), -jnp.inf, F32)
    qk(0, s_even)
    ms = _pair_trips(qi, pair, (m_init, m_init))
    qk(2 * qi + 1, s_odd)
    ms = consume(2 * qi, s_even, ms, first_key=0)
    consume(2 * qi + 1, s_odd, ms, first_key=tk)

    outs = [acc_ref[hl, 0:HEAD_DIM, :] / acc_ref[hl, HEAD_DIM:HEAD_DIM + 1, :] for hl in heads]
    o_ref[...] = jnp.concatenate(outs, axis=0).T.astype(o_ref.dtype)


def _attention(qa, ka, vt, tq=512):
    bsz, seq, _ = qa.shape
    nh = N_HEADS
    tk = tq // 2
    return pl.pallas_call(
        _attn_kernel,
        grid=(bsz, nh // 2, seq // tq),
        in_specs=[
            pl.BlockSpec((None, tq, 2 * AUG), lambda b, h, i: (b, i, h)),
            pl.BlockSpec((None, seq, 2 * AUG), lambda b, h, i: (b, 0, h)),
            pl.BlockSpec((None, 2, V_ROWS, seq), lambda b, h, i: (b, h, 0, 0)),
        ],
        out_specs=pl.BlockSpec((None, tq, 2 * HEAD_DIM), lambda b, h, i: (b, i, h)),
        out_shape=jax.ShapeDtypeStruct((bsz, seq, nh * HEAD_DIM), BF16),
        scratch_shapes=[pltpu.VMEM((2, tk, tq), F32), pltpu.VMEM((2, tk, tq), F32),
                        pltpu.VMEM((2, V_ROWS, tq), F32)],
        compiler_params=_params(("parallel", "parallel", "arbitrary"), 48),
        name="attn",
    )(qa, ka, vt)


def _oproj_kernel(x_ref, o_ref, wo_ref, g_ref, wr_ref, br_ref, x3_ref, h_ref, gates_ref, route_ref, cnt_ref, count):
    tm = x_ref.shape[0]

    @pl.when(pl.program_id(0) == 0)
    def _():
        count[...] = jnp.zeros_like(count)

    x3 = x_ref[...] + _dot(o_ref[...], wo_ref[...])
    x3_ref[...] = x3
    h = _rms(x3, g_ref[...])
    _store_token_major(h_ref, h)
    logits = _dot_split(h, wr_ref) + br_ref[...]
    lane = lax.broadcasted_iota(jnp.int32, logits.shape, 1)
    m1 = jnp.max(logits, axis=-1, keepdims=True)
    i1 = jnp.min(jnp.where(logits == m1, lane, LANES), axis=-1, keepdims=True)
    rest = jnp.where(lane == i1, -jnp.inf, logits)
    m2 = jnp.max(rest, axis=-1, keepdims=True)
    i2 = jnp.min(jnp.where(rest == m2, lane, LANES), axis=-1, keepdims=True)
    e = jnp.exp(m2 - m1)
    den = 1.0 + e
    gates_ref[...] = jnp.where(lane == 0, 1.0 / den, jnp.where(lane == 1, e / den, 0.0))

    hot = (lane == i1) | (lane == i2)
    r_i = lax.broadcasted_iota(jnp.int32, (tm, tm), 0)
    c_i = lax.broadcasted_iota(jnp.int32, (tm, tm), 1)
    rank = _dot((c_i < r_i).astype(BF16), hot.astype(BF16)) + count[0:1, :]
    count[0:1, :] = rank[tm - 1:tm, :] + hot[tm - 1:tm, :].astype(F32)
    cnt_ref[...] = count[...]
    rank0 = jnp.sum(jnp.where(lane == i1, rank, 0.0), axis=-1, keepdims=True)
    rank1 = jnp.sum(jnp.where(lane == i2, rank, 0.0), axis=-1, keepdims=True)
    packed = jnp.where(lane == 0, rank0, jnp.where(lane == 1, rank1, jnp.where(lane == 2, i1.astype(F32),
                                                                             jnp.where(lane == 3, i2.astype(F32), 0.0))))
    route_ref[...] = packed.T[0:SUBLANES, :].astype(jnp.int32)


def _oproj_router(x, o, w_o, g, router_w, router_b, tm):
    t, d = x.shape
    ne = router_w.shape[-1]
    br = jnp.full((1, LANES), NEG_BIG, F32).at[0, :ne].set(router_b.astype(F32))
    const = lambda shape: pl.BlockSpec(shape, lambda i: (0,) * len(shape))
    return pl.pallas_call(
        _oproj_kernel,
        grid=(t // tm,),
        in_specs=[pl.BlockSpec((tm, d), lambda i: (i, 0)), pl.BlockSpec((tm, d), lambda i: (i, 0)),
                  const((d, d)), const((1, d)), const((d, 2 * LANES)), const((1, LANES))],
        out_specs=[pl.BlockSpec((tm, d), lambda i: (i, 0)), pl.BlockSpec((tm * SUBLANES, LANES), lambda i: (i, 0)),
                   pl.BlockSpec((tm, LANES), lambda i: (i, 0)),
                   pl.BlockSpec((None, SUBLANES, tm), lambda i: (i, 0, 0)),
                   const((SUBLANES, LANES))],
        out_shape=[jax.ShapeDtypeStruct((t, d), F32), jax.ShapeDtypeStruct((t * SUBLANES, LANES), F32),
                   jax.ShapeDtypeStruct((t, LANES), F32),
                   jax.ShapeDtypeStruct((t // tm, SUBLANES, tm), jnp.int32),
                   jax.ShapeDtypeStruct((SUBLANES, LANES), F32)],
        scratch_shapes=[pltpu.VMEM((SUBLANES, LANES), F32)],
        compiler_params=_params(("arbitrary",), 40),
        name="oproj_router",
    )(x, o, w_o.astype(BF16), g.reshape(1, d).astype(F32), _hi_lo(router_w), br)


def _token_copies(copy_of, pos_ref, sem, n_tokens):
    def issue(r, carry):
        for slot in range(TOP_K):
            pltpu.make_async_copy(*copy_of(r, pos_ref[slot, r], slot), sem).start(priority=slot % 2)
        return carry

    lax.fori_loop(0, n_tokens, issue, 0, unroll=8)


def _zero_fill(zeros, xs_ref, sem, first_ref, last_ref, n_spans, tm):
    bits = [1 << k for k in reversed(range(tm.bit_length() - 1))]

    def each(act):
        for i in range(n_spans):
            a = first_ref[i]
            length = last_ref[i] - a
            lax.fori_loop(0, length // tm, lambda j, c: act(a + j * tm, tm) or c, 0)
            done = length // tm * tm
            for nb in bits:
                @pl.when(length & nb != 0)
                def _():
                    act(a + done + (length % tm) // (2 * nb) * (2 * nb), nb)

    copy = lambda p, n: pltpu.make_async_copy(_token(zeros, 0, n), _token(xs_ref, p, n), sem)
    each(lambda p, n: copy(p, n).start())
    each(lambda p, n: copy(p, n).wait())


def _dispatch_kernel(n_spans, first_ref, last_ref, pos_ref, h_ref, xs_ref, zeros, sem):
    tm = h_ref.shape[0] // SUBLANES
    _token_copies(lambda r, p, slot: (_token(h_ref, r), _token(xs_ref, p)), pos_ref, sem, tm)
    for slot in range(TOP_K):
        pltpu.make_async_copy(h_ref, _token(xs_ref, 0, tm), sem).wait()

    @pl.when(pl.program_id(0) == pl.num_programs(0) - 1)
    def _():
        zeros[...] = jnp.zeros_like(zeros)
        _zero_fill(zeros, xs_ref, sem, first_ref, last_ref, n_spans, tm)


def _dispatch(h, pos, span_first, span_last, n_tokens, tm):
    assert tm & (tm - 1) == 0
    grid_spec = pltpu.PrefetchScalarGridSpec(
        num_scalar_prefetch=2,
        grid=(h.shape[0] // (tm * SUBLANES),),
        in_specs=[pl.BlockSpec((None, SUBLANES, tm), lambda i, a, b: (i, 0, 0), memory_space=pltpu.SMEM),
                  pl.BlockSpec((tm * SUBLANES, LANES), lambda i, a, b: (i, 0))],
        out_specs=pl.BlockSpec(memory_space=pl.ANY),
        scratch_shapes=[pltpu.VMEM((tm * SUBLANES, LANES), F32), pltpu.SemaphoreType.DMA],
    )
    return pl.pallas_call(
        functools.partial(_dispatch_kernel, span_first.shape[0]),
        grid_spec=grid_spec,
        out_shape=jax.ShapeDtypeStruct((n_tokens * SUBLANES, LANES), F32),
        compiler_params=_params(("arbitrary",), 24),
        name="moe_dispatch",
    )(span_first, span_last, pos, h)


def _experts_kernel(te_ref, nt_ref, x_ref, wg_ref, wu_ref, wd_ref, y_ref):
    tm = x_ref.shape[0] // SUBLANES
    live = pl.program_id(0) < nt_ref[0]

    @pl.when(live)
    def _():
        x = _load_token_major(x_ref, tm).astype(BF16)
        _store_token_major(y_ref, _swiglu(x, wg_ref, wu_ref, wd_ref, 11))

    @pl.when(jnp.logical_not(live))
    def _():
        y_ref[...] = jnp.zeros_like(y_ref)


def _experts(xs, tile_expert, n_tiles, w_gate, w_up, w_down, tm):
    ne, d, f = w_gate.shape
    expert = lambda g, te, nt: (te[g], 0, 0)
    rows = pl.BlockSpec((tm * SUBLANES, LANES), lambda g, te, nt: (g, 0))
    grid_spec = pltpu.PrefetchScalarGridSpec(
        num_scalar_prefetch=2,
        grid=(xs.shape[0] // (tm * SUBLANES),),
        in_specs=[rows, pl.BlockSpec((None, d, f), expert), pl.BlockSpec((None, d, f), expert),
                  pl.BlockSpec((None, f, d), expert)],
        out_specs=rows,
    )
    return pl.pallas_call(
        _experts_kernel,
        grid_spec=grid_spec,
        out_shape=jax.ShapeDtypeStruct(xs.shape, F32),
        compiler_params=_params(("arbitrary",), 60),
        name="moe_experts",
    )(tile_expert, n_tiles, xs, w_gate.astype(BF16), w_up.astype(BF16), w_down.astype(BF16))


def _combine_kernel(pos_ref, nxt_ref, x_ref, gates_ref, gf_ref, ys_ref, o_ref, ybuf, sems):
    tm = x_ref.shape[0]
    i = pl.program_id(0)
    half = lax.rem(i, 2)

    def fetch(p_ref, h):
        _token_copies(lambda r, p, slot: (_token(ys_ref, p), _token(ybuf.at[h, slot], r)), p_ref, sems.at[h], tm)

    @pl.when(i == 0)
    def _():
        fetch(pos_ref, 0)

    @pl.when(i + 1 < pl.num_programs(0))
    def _():
        fetch(nxt_ref, 1 - half)

    for slot in range(TOP_K):
        pltpu.make_async_copy(_token(ys_ref, 0, tm), ybuf.at[half, slot], sems.at[half]).wait()
    gates = gates_ref[...]
    mix = (gates[:, 0:1] * _load_token_major(ybuf.at[half, 0], tm)
           + gates[:, 1:2] * _load_token_major(ybuf.at[half, 1], tm))
    o_ref[...] = _rms(x_ref[...] + mix, gf_ref[...])


def _combine(x, gates, pos, ys, g_final, tm):
    t, d = x.shape
    n = t // tm
    routes = lambda imap: pl.BlockSpec((None, SUBLANES, tm), imap, memory_space=pltpu.SMEM)
    return pl.pallas_call(
        _combine_kernel,
        grid=(n,),
        in_specs=[routes(lambda i: (i, 0, 0)), routes(lambda i: (jnp.minimum(i + 1, n - 1), 0, 0)),
                  pl.BlockSpec((tm, d), lambda i: (i, 0)), pl.BlockSpec((tm, LANES), lambda i: (i, 0)),
                  pl.BlockSpec((1, d), lambda i: (0, 0)), pl.BlockSpec(memory_space=pl.ANY)],
        out_specs=pl.BlockSpec((tm, d), lambda i: (i, 0)),
        out_shape=jax.ShapeDtypeStruct((t, d), F32),
        scratch_shapes=[pltpu.VMEM((2, TOP_K, tm * SUBLANES, LANES), F32), pltpu.SemaphoreType.DMA((2,))],
        compiler_params=_params(("arbitrary",), 40),
        name="moe_combine",
    )(pos, pos, x, gates, g_final.reshape(1, d).astype(F32), ys)


def _moe(x3, h3, gates, route, counts, w_gate, w_up, w_down, g_final, tm):
    t, d = x3.shape
    ne = w_gate.shape[0]
    n_steps = TOP_K * t // tm + ne
    n = counts[0, :ne].astype(jnp.int32)
    tiles = (n + tm - 1) // tm
    first = jnp.cumsum(tiles) - tiles
    n_tiles = jnp.sum(tiles)
    live = jnp.minimum(jnp.arange(n_steps, dtype=jnp.int32), n_tiles - 1)
    tile_expert = (jnp.sum(live[:, None] >= first[None, :], axis=1) - 1).astype(jnp.int32)

    rank, eid = route[:, 0:TOP_K, :], route[:, TOP_K:2 * TOP_K, :]
    pos = jnp.pad(first[eid] * tm + rank, ((0, 0), (0, SUBLANES - TOP_K), (0, 0)))
    span_first = jnp.concatenate([first * tm + n, (n_tiles * tm)[None]]).astype(jnp.int32)
    span_last = jnp.concatenate([(first + tiles) * tm, jnp.full((1,), n_steps * tm)]).astype(jnp.int32)

    xs = _dispatch(h3, pos, span_first, span_last, n_steps * tm, tm)
    ys = _experts(xs, tile_expert, n_tiles.reshape(1).astype(jnp.int32), w_gate, w_up, w_down, tm)
    return _combine(x3, gates, pos, ys, g_final, tm)


def kernel(x, mix_norm, ffn_norm, conv_w_in, conv_b_in, conv_w_dw, conv_b_dw, conv_ln_g, conv_ln_b, conv_w_out,
           conv_b_out, kv_norm, w_kvf, b_f, w_q, w_o, ffn_w_gate, ffn_w_up, ffn_w_down, router_w, router_b,
           moe_w_gate, moe_w_up, moe_w_down, final_norm):
    bsz, seq, d = x.shape
    assert mix_norm.shape[0] == 2 and conv_w_in.shape[0] == 1 and w_q.shape[0] == 1 and moe_w_gate.shape[0] == 1
    assert d == N_HEADS * HEAD_DIM and conv_w_dw.shape[1] == CONV_WIDTH

    x1 = _mixer0(x, mix_norm[0], conv_w_in[0], conv_b_in[0], conv_w_dw[0], conv_b_dw[0], conv_ln_g[0],
                 conv_ln_b[0], conv_w_out[0], conv_b_out[0])
    x2 = _ffn0(x1.reshape(bsz * seq, d), ffn_norm[0], ffn_w_gate[0], ffn_w_up[0], ffn_w_down[0])
    qa, ka, vt = _qkv(x2.reshape(bsz, seq, d), mix_norm[1], kv_norm, w_q[0], w_kvf, b_f)
    o = _attention(qa, ka, vt)
    x3, h3, gates, route, counts = _oproj_router(x2, o.reshape(bsz * seq, d), w_o[0], ffn_norm[1], router_w[0],
                                                 router_b[0], MOE_ROWS)
    out = _moe(x3, h3, gates, route, counts, moe_w_gate[0], moe_w_up[0], moe_w_down[0], final_norm, MOE_ROWS)
    return out.reshape(bsz, seq, d)
```

```python
import functools

import jax
import jax.numpy as jnp
from jax import lax
from jax.experimental import pallas as pl
from jax.experimental.pallas import tpu as pltpu

F32 = jnp.float32
BF16 = jnp.bfloat16

EPS = 1e-6
N_HEADS = 16
HEAD_DIM = 64
CONV_WIDTH = 31
TOP_K = 2

LANES = 128
SUBLANES = 8
MIB = 1024 * 1024

HALO = 32
CONV_ROWS = 64
CONV_COLS = 256
AUG = 2 * HEAD_DIM
V_ROWS = HEAD_DIM + 16
MOE_ROWS = 512
NEG_BIG = -1e30
LOG2E = 1.4426950408889634


def _rms(x, g):
    return x * lax.rsqrt(jnp.mean(x * x, axis=-1, keepdims=True) + EPS) * g


def _dot(a, b):
    return jnp.dot(a, b, preferred_element_type=F32)


def _dot_nt(a, b):
    return lax.dot_general(a, b, (((1,), (1,)), ((), ())), preferred_element_type=F32)


def _split3(x):
    p0 = x.astype(BF16).astype(F32)
    r = x - p0
    p1 = r.astype(BF16).astype(F32)
    return p0, p1, (r - p1).astype(BF16).astype(F32)


def _hi_lo(w):
    w = jnp.pad(w.astype(F32), ((0, 0), (0, LANES - w.shape[1])))
    hi = w.astype(BF16)
    return jnp.concatenate([hi, (w - hi.astype(F32)).astype(BF16)], axis=1)


def _dot_split(x, w_ref):
    hi = x.astype(BF16)
    lo = (x - hi.astype(F32)).astype(BF16)
    both = _dot(hi, w_ref[...])
    return both[:, :LANES] + both[:, LANES:] + _dot(lo, w_ref[:, :LANES])


def _store_token_major(ref, x):
    n = x.shape[0]
    assert x.shape[1] == SUBLANES * LANES
    for s in range(SUBLANES):
        ref[pl.ds(s, n, stride=SUBLANES), :] = x[:, s * LANES:(s + 1) * LANES]


def _load_token_major(ref, n):
    return jnp.concatenate([ref[pl.ds(s, n, stride=SUBLANES), :] for s in range(SUBLANES)], axis=1)


def _token(ref, r, n=1):
    return ref.at[pl.ds(pl.multiple_of(r * SUBLANES, SUBLANES), n * SUBLANES)]


def _params(sem, vmem_mib):
    return pltpu.CompilerParams(dimension_semantics=sem, vmem_limit_bytes=vmem_mib * MIB)


def _mixer_kernel(x_ref, g_ref, win_ref, bin_ref, wdw_ref, bdw_ref, lng_ref, lnb_ref, wout_ref, bout_ref,
                  o_ref, ubuf, shifted, cbuf):
    tm, d = x_ref.shape

    @pl.when(pl.program_id(1) == 0)
    def _():
        ubuf[0:HALO, :] = jnp.zeros((HALO, d), F32)

    x = x_ref[...]
    h = _rms(x, g_ref[...]).astype(BF16)
    u = _dot(h, win_ref[...]) + bin_ref[...]
    ubuf[HALO:HALO + tm, :] = u[:, :d] * jax.nn.sigmoid(u[:, d:])

    n_sh = tm + HALO - SUBLANES
    for b in range(1, SUBLANES):
        shifted[b - 1, 0:n_sh, :] = ubuf[b:b + n_sh, :]

    first = HALO - (CONV_WIDTH - 1)

    n_blk = CONV_ROWS // SUBLANES
    phases = [[(k, (first + k) // SUBLANES) for k in range(CONV_WIDTH) if (first + k) % SUBLANES == b]
              for b in range(SUBLANES)]

    def conv_rows(i, carry):
        r0 = pl.multiple_of(i * CONV_ROWS, CONV_ROWS)
        for c in range(d // CONV_COLS):
            cols = slice(c * CONV_COLS, (c + 1) * CONV_COLS)
            accs = [jnp.zeros((SUBLANES, CONV_COLS), F32) for _ in range(n_blk)]
            for b, taps in enumerate(phases):
                src = ubuf if b == 0 else shifted.at[b - 1]
                wts = {k: jnp.broadcast_to(wdw_ref[k:k + 1, cols], (SUBLANES, CONV_COLS)) for k, _ in taps}
                for j in range(min(a for _, a in taps), n_blk + max(a for _, a in taps)):
                    win = src[pl.ds(r0 + j * SUBLANES, SUBLANES), cols]
                    for k, a in taps:
                        if 0 <= j - a < n_blk:
                            accs[j - a] = accs[j - a] + win * wts[k]
            for t in range(n_blk):
                cbuf[pl.ds(r0 + t * SUBLANES, SUBLANES), cols] = accs[t]
        return carry

    lax.fori_loop(0, tm // CONV_ROWS, conv_rows, 0)

    ubuf[0:HALO, :] = ubuf[tm:tm + HALO, :]

    c = cbuf[...] + bdw_ref[...]
    mu = jnp.mean(c, axis=-1, keepdims=True)
    cc = c - mu
    var = jnp.mean(cc * cc, axis=-1, keepdims=True)
    y = cc * lax.rsqrt(var + EPS) * lng_ref[...] + lnb_ref[...]
    y = (y * jax.nn.sigmoid(y)).astype(BF16)
    o_ref[...] = x + _dot(y, wout_ref[...]) + bout_ref[...]


def _mixer0(x, g, w_in, b_in, w_dw, b_dw, ln_g, ln_b, w_out, b_out, tm=512):
    bsz, seq, d = x.shape
    row = lambda a: a.reshape(1, -1).astype(F32)
    const = lambda shape: pl.BlockSpec(shape, lambda b, s: (0,) * len(shape))
    return pl.pallas_call(
        _mixer_kernel,
        grid=(bsz, seq // tm),
        in_specs=[
            pl.BlockSpec((None, tm, d), lambda b, s: (b, s, 0)),
            const((1, d)), const((d, 2 * d)), const((1, 2 * d)), const((CONV_WIDTH, d)), const((1, d)),
            const((1, d)), const((1, d)), const((d, d)), const((1, d)),
        ],
        out_specs=pl.BlockSpec((None, tm, d), lambda b, s: (b, s, 0)),
        out_shape=jax.ShapeDtypeStruct((bsz, seq, d), F32),
        scratch_shapes=[
            pltpu.VMEM((tm + HALO, d), F32),
            pltpu.VMEM((SUBLANES - 1, tm + HALO - SUBLANES, d), F32),
            pltpu.VMEM((tm, d), F32),
        ],
        compiler_params=_params(("parallel", "arbitrary"), 56),
        name="mixer0",
    )(x, row(g), w_in.astype(BF16), row(b_in), w_dw.astype(F32), row(b_dw), row(ln_g), row(ln_b),
      w_out.astype(BF16), row(b_out))


def _swiglu(h, wg_ref, wu_ref, wd_ref, n_chunks):
    f = wg_ref.shape[-1]
    fc = f // n_chunks
    y = None
    for c in range(n_chunks):
        cols = slice(c * fc, (c + 1) * fc)
        g = _dot(h, wg_ref[:, cols])
        u = _dot(h, wu_ref[:, cols])
        a = (g * jax.nn.sigmoid(g) * u).astype(BF16)
        part = _dot(a, wd_ref[cols, :])
        y = part if y is None else y + part
    return y


def _ffn_kernel(x_ref, g_ref, wg_ref, wu_ref, wd_ref, o_ref):
    x = x_ref[...]
    h = _rms(x, g_ref[...]).astype(BF16)
    o_ref[...] = x + _swiglu(h, wg_ref, wu_ref, wd_ref, 1)


def _ffn0(x, g, w_gate, w_up, w_down, tm=512):
    t, d = x.shape
    f = w_gate.shape[-1]
    const = lambda shape: pl.BlockSpec(shape, lambda i: (0,) * len(shape), pipeline_mode=pl.Buffered(1))
    return pl.pallas_call(
        _ffn_kernel,
        grid=(t // tm,),
        in_specs=[pl.BlockSpec((tm, d), lambda i: (i, 0)), const((1, d)), const((d, f)), const((d, f)),
                  const((f, d))],
        out_specs=pl.BlockSpec((tm, d), lambda i: (i, 0)),
        out_shape=jax.ShapeDtypeStruct((t, d), F32),
        compiler_params=_params(("parallel",), 56),
        name="ffn0",
    )(x, g.reshape(1, d).astype(F32), w_gate.astype(BF16), w_up.astype(BF16), w_down.astype(BF16))


def _qkv_kernel(x_ref, gq_ref, gkv_ref, wq_ref, wk_ref, wvt_ref, wf_ref, bf_ref, selq_ref, selk_ref, cq_ref,
                ck_ref, qa_ref, ka_ref, vt_ref, carry):
    tm, d = x_ref.shape

    @pl.when(pl.program_id(1) == 0)
    def _():
        carry[...] = jnp.zeros_like(carry)

    x = x_ref[...]
    xn = x * lax.rsqrt(jnp.mean(x * x, axis=-1, keepdims=True) + EPS)
    hq = (xn * gq_ref[...]).astype(BF16)
    hkv32 = xn * gkv_ref[...]
    hkv = hkv32.astype(BF16)

    f_logit = _dot_split(hkv32, wf_ref) + bf_ref[...]
    logsig = jnp.minimum(f_logit, 0.0) - jnp.log1p(jnp.exp(-jnp.abs(f_logit)))

    lane = lax.broadcasted_iota(jnp.int32, logsig.shape, 1)
    by_group = lambda p: jnp.where(lane < N_HEADS, p[0], jnp.where(lane < 2 * N_HEADS, p[1],
                                                                   jnp.where(lane < 3 * N_HEADS, p[2], 0.0)))
    r_i = lax.broadcasted_iota(jnp.int32, (tm, tm), 0)
    c_i = lax.broadcasted_iota(jnp.int32, (tm, tm), 1)
    tri = (c_i <= r_i).astype(BF16)
    cum = _dot(tri, by_group(_split3(logsig)).astype(BF16))
    for shift in (N_HEADS, 2 * N_HEADS, 4 * N_HEADS):
        cum = cum + pltpu.roll(cum, shift, axis=1)
    dcum = cum + carry[0:1, :]
    carry[0:1, :] = dcum[tm - 1:tm, :]

    pieces = by_group(_split3(dcum * LOG2E)).astype(BF16)
    xq = _dot(pieces, selq_ref[...]) + cq_ref[...]
    xk = _dot(pieces, selk_ref[...]) + ck_ref[...]

    q = _dot(hq, wq_ref[...]) * (HEAD_DIM ** -0.5 * LOG2E)
    k = _dot(hkv, wk_ref[...])
    lane_a = lax.broadcasted_iota(jnp.int32, (tm, AUG), 1)
    for p in range(N_HEADS // 2):
        src = slice(p * AUG, (p + 1) * AUG)
        for odd in range(2):
            dst = slice((2 * p + odd) * AUG, (2 * p + odd + 1) * AUG)
            qv, kv = q[:, src], k[:, src]
            if odd:
                qv, kv = pltpu.roll(qv, HEAD_DIM, axis=1), pltpu.roll(kv, HEAD_DIM, axis=1)
            out = slice(odd * AUG, (odd + 1) * AUG)
            qa_ref[p, :, out] = jnp.where(lane_a < HEAD_DIM, qv, xq[:, dst]).astype(BF16)
            ka_ref[p, :, out] = jnp.where(lane_a < HEAD_DIM, kv, xk[:, dst]).astype(BF16)

    vt = _dot_nt(wvt_ref[...], hkv).astype(BF16)
    row = lax.broadcasted_iota(jnp.int32, (V_ROWS - HEAD_DIM, tm), 0)
    ones_row = (row == 0).astype(BF16)
    for h in range(N_HEADS):
        vt_ref[h, 0:HEAD_DIM, :] = vt[h * HEAD_DIM:(h + 1) * HEAD_DIM, :]
        vt_ref[h, HEAD_DIM:V_ROWS, :] = ones_row


def _qkv(x, g_q, g_kv, w_q, w_kvf, b_f, tm=512):
    bsz, seq, d = x.shape
    nh = N_HEADS
    w_k = w_kvf[:, :d]
    w_vt = w_kvf[:, d:2 * d].T
    w_f = w_kvf[:, 2 * d:]
    assert 8 * nh == LANES
    w_f3 = _hi_lo(jnp.tile(w_f, (1, 3)))
    b_f3 = jnp.zeros((1, LANES), F32).at[0, :3 * nh].set(jnp.tile(b_f.astype(F32), 3))

    j = jnp.arange(3)[:, None]
    hh = jnp.arange(nh)[None, :]
    col = lambda i: (hh * AUG + HEAD_DIM + i).reshape(-1)
    src = (j * nh + hh).reshape(-1)
    selq = jnp.zeros((LANES, nh * AUG), BF16).at[src, col(j)].set(1)
    selk = jnp.zeros((LANES, nh * AUG), BF16).at[src, col(3 + j)].set(1)
    cq = jnp.zeros((1, nh * AUG), F32).at[0, col(3 + j)].set(-1.0)
    ck = jnp.zeros((1, nh * AUG), F32).at[0, col(j)].set(1.0)

    const = lambda shape: pl.BlockSpec(shape, lambda b, s: (0,) * len(shape))
    return pl.pallas_call(
        _qkv_kernel,
        grid=(bsz, seq // tm),
        in_specs=[
            pl.BlockSpec((None, tm, d), lambda b, s: (b, s, 0)),
            const((1, d)), const((1, d)), const((d, d)), const((d, d)), const((d, d)), const((d, 2 * LANES)),
            const((1, LANES)), const((LANES, nh * AUG)), const((LANES, nh * AUG)), const((1, nh * AUG)),
            const((1, nh * AUG)),
        ],
        out_specs=[
            pl.BlockSpec((None, nh // 2, tm, 2 * AUG), lambda b, s: (b, 0, s, 0)),
            pl.BlockSpec((None, nh // 2, tm, 2 * AUG), lambda b, s: (b, 0, s, 0)),
            pl.BlockSpec((None, nh, V_ROWS, tm), lambda b, s: (b, 0, 0, s)),
        ],
        out_shape=[
            jax.ShapeDtypeStruct((bsz, nh // 2, seq, 2 * AUG), BF16),
            jax.ShapeDtypeStruct((bsz, nh // 2, seq, 2 * AUG), BF16),
            jax.ShapeDtypeStruct((bsz, nh, V_ROWS, seq), BF16),
        ],
        scratch_shapes=[pltpu.VMEM((SUBLANES, LANES), F32)],
        compiler_params=_params(("parallel", "arbitrary"), 56),
        name="qkv",
    )(x, g_q.reshape(1, d).astype(F32), g_kv.reshape(1, d).astype(F32), w_q.astype(BF16), w_k.astype(BF16),
      w_vt.astype(BF16), w_f3, b_f3, selq, selk, cq, ck)


def _pair_trips(n_pairs, pair, carry):
    for n in (4, 2, 1):
        trip = lambda i, c, n=n: functools.reduce(lambda c, k: pair(n * i + k, c), range(n), c)
        carry = lax.fori_loop(n_pairs // (2 * n) * 2 if n < 4 else 0, n_pairs // n, trip, carry)
    return carry


def _attn_kernel(qa_ref, ka_ref, vt_ref, o_ref, s_even, s_odd, acc_ref):
    tq = qa_ref.shape[0]
    tk = s_even.shape[1]
    qi = pl.program_id(2)
    heads = range(2)
    qs = [qa_ref[:, hl * AUG:(hl + 1) * AUG] for hl in heads]

    def rows(blk):
        return pl.ds(pl.multiple_of(blk * tk, tk), tk)

    def scores(blk, hl):
        return _dot_nt(ka_ref[rows(blk), hl * AUG:(hl + 1) * AUG], qs[hl])

    def causal(s, first_key):
        key = first_key + lax.broadcasted_iota(jnp.int32, s.shape, 0)
        qry = lax.broadcasted_iota(jnp.int32, s.shape, 1)
        return jnp.where(key <= qry, s, -jnp.inf)

    def qk(blk, s_ref):
        for hl in heads:
            s_ref[hl] = scores(blk, hl)

    def consume(blk, s_ref, ms, first_key=None):
        out = []
        for hl in heads:
            s = s_ref[hl] if first_key is None else causal(s_ref[hl], first_key)
            m_new = jnp.maximum(ms[hl], jnp.max(s, axis=0, keepdims=True))
            p = jnp.exp2(s - m_new).astype(BF16)
            acc_ref[hl] = jnp.exp2(ms[hl] - m_new) * acc_ref[hl] + _dot(vt_ref[hl, :, rows(blk)], p)
            out.append(m_new)
        return tuple(out)

    def pair(j, ms):
        qk(2 * j + 1, s_odd)
        ms = consume(2 * j, s_even, ms)
        qk(2 * j + 2, s_even)
        return consume(2 * j + 1, s_odd, ms)

    acc_ref[...] = jnp.zeros_like(acc_ref)
    m_init = jnp.full((1, tq), -jnp.inf, F32)
    qk(0, s_even)
    ms = _pair_trips(qi, pair, (m_init, m_init))
    qk(2 * qi + 1, s_odd)
    ms = consume(2 * qi, s_even, ms, first_key=0)
    consume(2 * qi + 1, s_odd, ms, first_key=tk)

    outs = [acc_ref[hl, 0:HEAD_DIM, :] / acc_ref[hl, HEAD_DIM:HEAD_DIM + 1, :] for hl in heads]
    o_ref[...] = jnp.concatenate(outs, axis=0).T.astype(o_ref.dtype)


def _attention(qa, ka, vt, tq=512):
    bsz, _, seq, _ = qa.shape
    nh = N_HEADS
    tk = tq // 2
    return pl.pallas_call(
        _attn_kernel,
        grid=(bsz, nh // 2, seq // tq),
        in_specs=[
            pl.BlockSpec((None, None, tq, 2 * AUG), lambda b, h, i: (b, h, i, 0)),
            pl.BlockSpec((None, None, seq, 2 * AUG), lambda b, h, i: (b, h, 0, 0)),
            pl.BlockSpec((None, 2, V_ROWS, seq), lambda b, h, i: (b, h, 0, 0)),
        ],
        out_specs=pl.BlockSpec((None, None, tq, 2 * HEAD_DIM), lambda b, h, i: (b, h, i, 0)),
        out_shape=jax.ShapeDtypeStruct((bsz, nh // 2, seq, 2 * HEAD_DIM), BF16),
        scratch_shapes=[pltpu.VMEM((2, tk, tq), F32), pltpu.VMEM((2, tk, tq), F32),
                        pltpu.VMEM((2, V_ROWS, tq), F32)],
        compiler_params=_params(("parallel", "parallel", "arbitrary"), 48),
        name="attn",
    )(qa, ka, vt)


def _oproj_kernel(x_ref, o_ref, wo_ref, g_ref, wr_ref, br_ref, x3_ref, h_ref, gates_ref, route_ref, cnt_ref, count):
    tm = x_ref.shape[0]

    @pl.when(pl.program_id(0) == 0)
    def _():
        count[...] = jnp.zeros_like(count)

    o = jnp.concatenate([o_ref[p] for p in range(o_ref.shape[0])], axis=1)
    x3 = x_ref[...] + _dot(o, wo_ref[...])
    x3_ref[...] = x3
    h = _rms(x3, g_ref[...])
    _store_token_major(h_ref, h)
    logits = _dot_split(h, wr_ref) + br_ref[...]
    lane = lax.broadcasted_iota(jnp.int32, logits.shape, 1)
    m1 = jnp.max(logits, axis=-1, keepdims=True)
    i1 = jnp.min(jnp.where(logits == m1, lane, LANES), axis=-1, keepdims=True)
    rest = jnp.where(lane == i1, -jnp.inf, logits)
    m2 = jnp.max(rest, axis=-1, keepdims=True)
    i2 = jnp.min(jnp.where(rest == m2, lane, LANES), axis=-1, keepdims=True)
    e = jnp.exp(m2 - m1)
    den = 1.0 + e
    gates_ref[...] = jnp.where(lane == 0, 1.0 / den, jnp.where(lane == 1, e / den, 0.0))

    hot = (lane == i1) | (lane == i2)
    r_i = lax.broadcasted_iota(jnp.int32, (tm, tm), 0)
    c_i = lax.broadcasted_iota(jnp.int32, (tm, tm), 1)
    rank = _dot((c_i < r_i).astype(BF16), hot.astype(BF16)) + count[0:1, :]
    count[0:1, :] = rank[tm - 1:tm, :] + hot[tm - 1:tm, :].astype(F32)
    cnt_ref[...] = count[...]
    rank0 = jnp.sum(jnp.where(lane == i1, rank, 0.0), axis=-1, keepdims=True)
    rank1 = jnp.sum(jnp.where(lane == i2, rank, 0.0), axis=-1, keepdims=True)
    packed = jnp.where(lane == 0, rank0, jnp.where(lane == 1, rank1, jnp.where(lane == 2, i1.astype(F32),
                                                                             jnp.where(lane == 3, i2.astype(F32), 0.0))))
    route_ref[...] = packed.T[0:SUBLANES, :].astype(jnp.int32)


def _oproj_router(x, o, w_o, g, router_w, router_b, tm):
    t, d = x.shape
    ne = router_w.shape[-1]
    br = jnp.full((1, LANES), NEG_BIG, F32).at[0, :ne].set(router_b.astype(F32))
    const = lambda shape: pl.BlockSpec(shape, lambda i: (0,) * len(shape))
    _, n_pairs, seq, pair_w = o.shape
    assert seq % tm == 0
    per_seq = seq // tm
    return pl.pallas_call(
        _oproj_kernel,
        grid=(t // tm,),
        in_specs=[pl.BlockSpec((tm, d), lambda i: (i, 0)),
                  pl.BlockSpec((None, n_pairs, tm, pair_w), lambda i: (i // per_seq, 0, i % per_seq, 0)),
                  const((d, d)), const((1, d)), const((d, 2 * LANES)), const((1, LANES))],
        out_specs=[pl.BlockSpec((tm, d), lambda i: (i, 0)), pl.BlockSpec((tm * SUBLANES, LANES), lambda i: (i, 0)),
                   pl.BlockSpec((tm, LANES), lambda i: (i, 0)),
                   pl.BlockSpec((None, SUBLANES, tm), lambda i: (i, 0, 0)),
                   const((SUBLANES, LANES))],
        out_shape=[jax.ShapeDtypeStruct((t, d), F32), jax.ShapeDtypeStruct((t * SUBLANES, LANES), F32),
                   jax.ShapeDtypeStruct((t, LANES), F32),
                   jax.ShapeDtypeStruct((t // tm, SUBLANES, tm), jnp.int32),
                   jax.ShapeDtypeStruct((SUBLANES, LANES), F32)],
        scratch_shapes=[pltpu.VMEM((SUBLANES, LANES), F32)],
        compiler_params=_params(("arbitrary",), 40),
        name="oproj_router",
    )(x, o, w_o.astype(BF16), g.reshape(1, d).astype(F32), _hi_lo(router_w), br)


def _token_copies(copy_of, pos_ref, sem, n_tokens):
    def issue(r, carry):
        for slot in range(TOP_K):
            pltpu.make_async_copy(*copy_of(r, pos_ref[slot, r], slot), sem).start(priority=slot % 2)
        return carry

    lax.fori_loop(0, n_tokens, issue, 0, unroll=8)


def _zero_fill(zeros, xs_ref, sem, first_ref, last_ref, n_spans, tm):
    bits = [1 << k for k in reversed(range(tm.bit_length() - 1))]

    def each(act):
        for i in range(n_spans):
            a = first_ref[i]
            length = last_ref[i] - a
            lax.fori_loop(0, length // tm, lambda j, c: act(a + j * tm, tm) or c, 0)
            done = length // tm * tm
            for nb in bits:
                @pl.when(length & nb != 0)
                def _():
                    act(a + done + (length % tm) // (2 * nb) * (2 * nb), nb)

    copy = lambda p, n: pltpu.make_async_copy(_token(zeros, 0, n), _token(xs_ref, p, n), sem)
    each(lambda p, n: copy(p, n).start())
    each(lambda p, n: copy(p, n).wait())


def _dispatch_kernel(n_spans, first_ref, last_ref, pos_ref, h_ref, xs_ref, zeros, sem):
    tm = h_ref.shape[0] // SUBLANES
    _token_copies(lambda r, p, slot: (_token(h_ref, r), _token(xs_ref, p)), pos_ref, sem, tm)
    for slot in range(TOP_K):
        pltpu.make_async_copy(h_ref, _token(xs_ref, 0, tm), sem).wait()

    @pl.when(pl.program_id(0) == pl.num_programs(0) - 1)
    def _():
        zeros[...] = jnp.zeros_like(zeros)
        _zero_fill(zeros, xs_ref, sem, first_ref, last_ref, n_spans, tm)


def _dispatch(h, pos, span_first, span_last, n_tokens, tm):
    assert tm & (tm - 1) == 0
    grid_spec = pltpu.PrefetchScalarGridSpec(
        num_scalar_prefetch=2,
        grid=(h.shape[0] // (tm * SUBLANES),),
        in_specs=[pl.BlockSpec((None, SUBLANES, tm), lambda i, a, b: (i, 0, 0), memory_space=pltpu.SMEM),
                  pl.BlockSpec((tm * SUBLANES, LANES), lambda i, a, b: (i, 0))],
        out_specs=pl.BlockSpec(memory_space=pl.ANY),
        scratch_shapes=[pltpu.VMEM((tm * SUBLANES, LANES), F32), pltpu.SemaphoreType.DMA],
    )
    return pl.pallas_call(
        functools.partial(_dispatch_kernel, span_first.shape[0]),
        grid_spec=grid_spec,
        out_shape=jax.ShapeDtypeStruct((n_tokens * SUBLANES, LANES), F32),
        compiler_params=_params(("arbitrary",), 24),
        name="moe_dispatch",
    )(span_first, span_last, pos, h)


def _experts_kernel(te_ref, nt_ref, x_ref, wg_ref, wu_ref, wd_ref, y_ref):
    tm = x_ref.shape[0] // SUBLANES
    live = pl.program_id(0) < nt_ref[0]

    @pl.when(live)
    def _():
        x = _load_token_major(x_ref, tm).astype(BF16)
        _store_token_major(y_ref, _swiglu(x, wg_ref, wu_ref, wd_ref, 11))

    @pl.when(jnp.logical_not(live))
    def _():
        y_ref[...] = jnp.zeros_like(y_ref)


def _experts(xs, tile_expert, n_tiles, w_gate, w_up, w_down, tm):
    ne, d, f = w_gate.shape
    expert = lambda g, te, nt: (te[g], 0, 0)
    rows = pl.BlockSpec((tm * SUBLANES, LANES), lambda g, te, nt: (g, 0))
    grid_spec = pltpu.PrefetchScalarGridSpec(
        num_scalar_prefetch=2,
        grid=(xs.shape[0] // (tm * SUBLANES),),
        in_specs=[rows, pl.BlockSpec((None, d, f), expert), pl.BlockSpec((None, d, f), expert),
                  pl.BlockSpec((None, f, d), expert)],
        out_specs=rows,
    )
    return pl.pallas_call(
        _experts_kernel,
        grid_spec=grid_spec,
        out_shape=jax.ShapeDtypeStruct(xs.shape, F32),
        compiler_params=_params(("arbitrary",), 60),
        name="moe_experts",
    )(tile_expert, n_tiles, xs, w_gate.astype(BF16), w_up.astype(BF16), w_down.astype(BF16))


def _combine_kernel(pos_ref, nxt_ref, x_ref, gates_ref, gf_ref, ys_ref, o_ref, ybuf, sems):
    tm = x_ref.shape[0]
    i = pl.program_id(0)
    half = lax.rem(i, 2)

    def fetch(p_ref, h):
        _token_copies(lambda r, p, slot: (_token(ys_ref, p), _token(ybuf.at[h, slot], r)), p_ref, sems.at[h], tm)

    @pl.when(i == 0)
    def _():
        fetch(pos_ref, 0)

    @pl.when(i + 1 < pl.num_programs(0))
    def _():
        fetch(nxt_ref, 1 - half)

    for slot in range(TOP_K):
        pltpu.make_async_copy(_token(ys_ref, 0, tm), ybuf.at[half, slot], sems.at[half]).wait()
    gates = gates_ref[...]
    mix = (gates[:, 0:1] * _load_token_major(ybuf.at[half, 0], tm)
           + gates[:, 1:2] * _load_token_major(ybuf.at[half, 1], tm))
    o_ref[...] = _rms(x_ref[...] + mix, gf_ref[...])


def _combine(x, gates, pos, ys, g_final, tm):
    t, d = x.shape
    n = t // tm
    routes = lambda imap: pl.BlockSpec((None, SUBLANES, tm), imap, memory_space=pltpu.SMEM)
    return pl.pallas_call(
        _combine_kernel,
        grid=(n,),
        in_specs=[routes(lambda i: (i, 0, 0)), routes(lambda i: (jnp.minimum(i + 1, n - 1), 0, 0)),
                  pl.BlockSpec((tm, d), lambda i: (i, 0)), pl.BlockSpec((tm, LANES), lambda i: (i, 0)),
                  pl.BlockSpec((1, d), lambda i: (0, 0)), pl.BlockSpec(memory_space=pl.ANY)],
        out_specs=pl.BlockSpec((tm, d), lambda i: (i, 0)),
        out_shape=jax.ShapeDtypeStruct((t, d), F32),
        scratch_shapes=[pltpu.VMEM((2, TOP_K, tm * SUBLANES, LANES), F32), pltpu.SemaphoreType.DMA((2,))],
        compiler_params=_params(("arbitrary",), 40),
        name="moe_combine",
    )(pos, pos, x, gates, g_final.reshape(1, d).astype(F32), ys)


def _moe(x3, h3, gates, route, counts, w_gate, w_up, w_down, g_final, tm):
    t, d = x3.shape
    ne = w_gate.shape[0]
    n_steps = TOP_K * t // tm + ne
    n = counts[0, :ne].astype(jnp.int32)
    tiles = (n + tm - 1) // tm
    first = jnp.cumsum(tiles) - tiles
    n_tiles = jnp.sum(tiles)
    live = jnp.minimum(jnp.arange(n_steps, dtype=jnp.int32), n_tiles - 1)
    tile_expert = (jnp.sum(live[:, None] >= first[None, :], axis=1) - 1).astype(jnp.int32)

    rank, eid = route[:, 0:TOP_K, :], route[:, TOP_K:2 * TOP_K, :]
    pos = jnp.pad(first[eid] * tm + rank, ((0, 0), (0, SUBLANES - TOP_K), (0, 0)))
    span_first = jnp.concatenate([first * tm + n, (n_tiles * tm)[None]]).astype(jnp.int32)
    span_last = jnp.concatenate([(first + tiles) * tm, jnp.full((1,), n_steps * tm)]).astype(jnp.int32)

    xs = _dispatch(h3, pos, span_first, span_last, n_steps * tm, tm)
    ys = _experts(xs, tile_expert, n_tiles.reshape(1).astype(jnp.int32), w_gate, w_up, w_down, tm)
    return _combine(x3, gates, pos, ys, g_final, tm)


def kernel(x, mix_norm, ffn_norm, conv_w_in, conv_b_in, conv_w_dw, conv_b_dw, conv_ln_g, conv_ln_b, conv_w_out,
           conv_b_out, kv_norm, w_kvf, b_f, w_q, w_o, ffn_w_gate, ffn_w_up, ffn_w_down, router_w, router_b,
           moe_w_gate, moe_w_up, moe_w_down, final_norm):
    bsz, seq, d = x.shape
    assert mix_norm.shape[0] == 2 and conv_w_in.shape[0] == 1 and w_q.shape[0] == 1 and moe_w_gate.shape[0] == 1
    assert d == N_HEADS * HEAD_DIM and conv_w_dw.shape[1] == CONV_WIDTH

    x1 = _mixer0(x, mix_norm[0], conv_w_in[0], conv_b_in[0], conv_w_dw[0], conv_b_dw[0], conv_ln_g[0],
                 conv_ln_b[0], conv_w_out[0], conv_b_out[0])
    x2 = _ffn0(x1.reshape(bsz * seq, d), ffn_norm[0], ffn_w_gate[0], ffn_w_up[0], ffn_w_down[0])
    qa, ka, vt = _qkv(x2.reshape(bsz, seq, d), mix_norm[1], kv_norm, w_q[0], w_kvf, b_f)
    o = _attention(qa, ka, vt)
    x3, h3, gates, route, counts = _oproj_router(x2, o, w_o[0], ffn_norm[1], router_w[0],
                                                 router_b[0], MOE_ROWS)
    out = _moe(x3, h3, gates, route, counts, moe_w_gate[0], moe_w_up[0], moe_w_down[0], final_norm, MOE_ROWS)
    return out.reshape(bsz, seq, d)
```

```python
import functools

import jax
import jax.numpy as jnp
from jax import lax
from jax.experimental import pallas as pl
from jax.experimental.pallas import tpu as pltpu

F32 = jnp.float32
BF16 = jnp.bfloat16

EPS = 1e-6
N_HEADS = 16
HEAD_DIM = 64
CONV_WIDTH = 31
TOP_K = 2

LANES = 128
SUBLANES = 8
MIB = 1024 * 1024

HALO = 32
CONV_ROWS = 64
CONV_COLS = 256
AUG = 2 * HEAD_DIM
V_ROWS = HEAD_DIM + 16
MOE_ROWS = 512
NEG_BIG = -1e30
LOG2E = 1.4426950408889634


def _rms(x, g):
    return x * lax.rsqrt(jnp.mean(x * x, axis=-1, keepdims=True) + EPS) * g


def _dot(a, b):
    return jnp.dot(a, b, preferred_element_type=F32)


def _dot_nt(a, b):
    return lax.dot_general(a, b, (((1,), (1,)), ((), ())), preferred_element_type=F32)


def _split3(x):
    p0 = x.astype(BF16).astype(F32)
    r = x - p0
    p1 = r.astype(BF16).astype(F32)
    return p0, p1, (r - p1).astype(BF16).astype(F32)


def _hi_lo(w):
    w = jnp.pad(w.astype(F32), ((0, 0), (0, LANES - w.shape[1])))
    hi = w.astype(BF16)
    return jnp.concatenate([hi, (w - hi.astype(F32)).astype(BF16)], axis=1)


def _dot_split(x, w_ref):
    hi = x.astype(BF16)
    lo = (x - hi.astype(F32)).astype(BF16)
    both = _dot(hi, w_ref[...])
    return both[:, :LANES] + both[:, LANES:] + _dot(lo, w_ref[:, :LANES])


def _store_token_major(ref, x):
    n = x.shape[0]
    assert x.shape[1] == SUBLANES * LANES
    for s in range(SUBLANES):
        ref[pl.ds(s, n, stride=SUBLANES), :] = x[:, s * LANES:(s + 1) * LANES]


def _load_token_major(ref, n):
    return jnp.concatenate([ref[pl.ds(s, n, stride=SUBLANES), :] for s in range(SUBLANES)], axis=1)


def _token(ref, r, n=1):
    return ref.at[pl.ds(pl.multiple_of(r * SUBLANES, SUBLANES), n * SUBLANES)]


def _params(sem, vmem_mib):
    return pltpu.CompilerParams(dimension_semantics=sem, vmem_limit_bytes=vmem_mib * MIB)


def _mixer_kernel(x_ref, g_ref, win_ref, bin_ref, wdw_ref, bdw_ref, lng_ref, lnb_ref, wout_ref, bout_ref,
                  o_ref, ubuf, shifted, cbuf):
    tm, d = x_ref.shape

    @pl.when(pl.program_id(1) == 0)
    def _():
        ubuf[0:HALO, :] = jnp.zeros((HALO, d), F32)

    x = x_ref[...]
    h = _rms(x, g_ref[...]).astype(BF16)
    u = _dot(h, win_ref[...]) + bin_ref[...]
    ubuf[HALO:HALO + tm, :] = u[:, :d] * jax.nn.sigmoid(u[:, d:])

    n_sh = tm + HALO - SUBLANES
    for b in range(1, SUBLANES):
        shifted[b - 1, 0:n_sh, :] = ubuf[b:b + n_sh, :]

    first = HALO - (CONV_WIDTH - 1)

    n_blk = CONV_ROWS // SUBLANES
    phases = [[(k, (first + k) // SUBLANES) for k in range(CONV_WIDTH) if (first + k) % SUBLANES == b]
              for b in range(SUBLANES)]

    def conv_rows(i, carry):
        r0 = pl.multiple_of(i * CONV_ROWS, CONV_ROWS)
        for c in range(d // CONV_COLS):
            cols = slice(c * CONV_COLS, (c + 1) * CONV_COLS)
            accs = [jnp.zeros((SUBLANES, CONV_COLS), F32) for _ in range(n_blk)]
            for b, taps in enumerate(phases):
                src = ubuf if b == 0 else shifted.at[b - 1]
                wts = {k: jnp.broadcast_to(wdw_ref[k:k + 1, cols], (SUBLANES, CONV_COLS)) for k, _ in taps}
                for j in range(min(a for _, a in taps), n_blk + max(a for _, a in taps)):
                    win = src[pl.ds(r0 + j * SUBLANES, SUBLANES), cols]
                    for k, a in taps:
                        if 0 <= j - a < n_blk:
                            accs[j - a] = accs[j - a] + win * wts[k]
            for t in range(n_blk):
                cbuf[pl.ds(r0 + t * SUBLANES, SUBLANES), cols] = accs[t]
        return carry

    lax.fori_loop(0, tm // CONV_ROWS, conv_rows, 0)

    ubuf[0:HALO, :] = ubuf[tm:tm + HALO, :]

    c = cbuf[...] + bdw_ref[...]
    mu = jnp.mean(c, axis=-1, keepdims=True)
    cc = c - mu
    var = jnp.mean(cc * cc, axis=-1, keepdims=True)
    y = cc * lax.rsqrt(var + EPS) * lng_ref[...] + lnb_ref[...]
    y = (y * jax.nn.sigmoid(y)).astype(BF16)
    o_ref[...] = x + _dot(y, wout_ref[...]) + bout_ref[...]


def _mixer0(x, g, w_in, b_in, w_dw, b_dw, ln_g, ln_b, w_out, b_out, tm=512):
    bsz, seq, d = x.shape
    row = lambda a: a.reshape(1, -1).astype(F32)
    const = lambda shape: pl.BlockSpec(shape, lambda b, s: (0,) * len(shape))
    return pl.pallas_call(
        _mixer_kernel,
        grid=(bsz, seq // tm),
        in_specs=[
            pl.BlockSpec((None, tm, d), lambda b, s: (b, s, 0)),
            const((1, d)), const((d, 2 * d)), const((1, 2 * d)), const((CONV_WIDTH, d)), const((1, d)),
            const((1, d)), const((1, d)), const((d, d)), const((1, d)),
        ],
        out_specs=pl.BlockSpec((None, tm, d), lambda b, s: (b, s, 0)),
        out_shape=jax.ShapeDtypeStruct((bsz, seq, d), F32),
        scratch_shapes=[
            pltpu.VMEM((tm + HALO, d), F32),
            pltpu.VMEM((SUBLANES - 1, tm + HALO - SUBLANES, d), F32),
            pltpu.VMEM((tm, d), F32),
        ],
        compiler_params=_params(("parallel", "arbitrary"), 56),
        name="mixer0",
    )(x, row(g), w_in.astype(BF16), row(b_in), w_dw.astype(F32), row(b_dw), row(ln_g), row(ln_b),
      w_out.astype(BF16), row(b_out))


def _swiglu(h, wg_ref, wu_ref, wd_ref, n_chunks):
    f = wg_ref.shape[-1]
    fc = f // n_chunks
    y = None
    for c in range(n_chunks):
        cols = slice(c * fc, (c + 1) * fc)
        g = _dot(h, wg_ref[:, cols])
        u = _dot(h, wu_ref[:, cols])
        a = (g * jax.nn.sigmoid(g) * u).astype(BF16)
        part = _dot(a, wd_ref[cols, :])
        y = part if y is None else y + part
    return y


def _ffn_kernel(x_ref, g_ref, wg_ref, wu_ref, wd_ref, o_ref):
    x = x_ref[...]
    h = _rms(x, g_ref[...]).astype(BF16)
    o_ref[...] = x + _swiglu(h, wg_ref, wu_ref, wd_ref, 1)


def _ffn0(x, g, w_gate, w_up, w_down, tm=512):
    t, d = x.shape
    f = w_gate.shape[-1]
    const = lambda shape: pl.BlockSpec(shape, lambda i: (0,) * len(shape), pipeline_mode=pl.Buffered(1))
    return pl.pallas_call(
        _ffn_kernel,
        grid=(t // tm,),
        in_specs=[pl.BlockSpec((tm, d), lambda i: (i, 0)), const((1, d)), const((d, f)), const((d, f)),
                  const((f, d))],
        out_specs=pl.BlockSpec((tm, d), lambda i: (i, 0)),
        out_shape=jax.ShapeDtypeStruct((t, d), F32),
        compiler_params=_params(("parallel",), 56),
        name="ffn0",
    )(x, g.reshape(1, d).astype(F32), w_gate.astype(BF16), w_up.astype(BF16), w_down.astype(BF16))


def _qkv_kernel(x_ref, gq_ref, gkv_ref, wq_ref, wk_ref, wvt_ref, wf_ref, bf_ref, selq_ref, selk_ref, cq_ref,
                ck_ref, qt_ref, ka_ref, vt_ref, carry):
    tm, d = x_ref.shape

    @pl.when(pl.program_id(1) == 0)
    def _():
        carry[...] = jnp.zeros_like(carry)

    x = x_ref[...]
    xn = x * lax.rsqrt(jnp.mean(x * x, axis=-1, keepdims=True) + EPS)
    hq = (xn * gq_ref[...]).astype(BF16)
    hkv32 = xn * gkv_ref[...]
    hkv = hkv32.astype(BF16)

    f_logit = _dot_split(hkv32, wf_ref) + bf_ref[...]
    logsig = jnp.minimum(f_logit, 0.0) - jnp.log1p(jnp.exp(-jnp.abs(f_logit)))

    lane = lax.broadcasted_iota(jnp.int32, logsig.shape, 1)
    by_group = lambda p: jnp.where(lane < N_HEADS, p[0], jnp.where(lane < 2 * N_HEADS, p[1],
                                                                   jnp.where(lane < 3 * N_HEADS, p[2], 0.0)))
    r_i = lax.broadcasted_iota(jnp.int32, (tm, tm), 0)
    c_i = lax.broadcasted_iota(jnp.int32, (tm, tm), 1)
    tri = (c_i <= r_i).astype(BF16)
    cum = _dot(tri, by_group(_split3(logsig)).astype(BF16))
    for shift in (N_HEADS, 2 * N_HEADS, 4 * N_HEADS):
        cum = cum + pltpu.roll(cum, shift, axis=1)
    dcum = cum + carry[0:1, :]
    carry[0:1, :] = dcum[tm - 1:tm, :]

    pieces = by_group(_split3(dcum * LOG2E)).astype(BF16)
    xq = _dot(pieces, selq_ref[...]) + cq_ref[...]
    xk = _dot(pieces, selk_ref[...]) + ck_ref[...]

    q = _dot(hq, wq_ref[...]) * (HEAD_DIM ** -0.5 * LOG2E)
    k = _dot(hkv, wk_ref[...])
    lane_a = lax.broadcasted_iota(jnp.int32, (tm, AUG), 1)
    for p in range(N_HEADS // 2):
        src = slice(p * AUG, (p + 1) * AUG)
        q_pair = []
        for odd in range(2):
            dst = slice((2 * p + odd) * AUG, (2 * p + odd + 1) * AUG)
            qv, kv = q[:, src], k[:, src]
            if odd:
                qv, kv = pltpu.roll(qv, HEAD_DIM, axis=1), pltpu.roll(kv, HEAD_DIM, axis=1)
            out = slice(odd * AUG, (odd + 1) * AUG)
            q_pair.append(jnp.where(lane_a < HEAD_DIM, qv, xq[:, dst]))
            ka_ref[p, :, out] = jnp.where(lane_a < HEAD_DIM, kv, xk[:, dst]).astype(BF16)
        qt_ref[p] = jnp.concatenate(q_pair, axis=1).T.astype(BF16)

    vt = _dot_nt(wvt_ref[...], hkv).astype(BF16)
    row = lax.broadcasted_iota(jnp.int32, (V_ROWS - HEAD_DIM, tm), 0)
    ones_row = (row == 0).astype(BF16)
    for h in range(N_HEADS):
        vt_ref[h, 0:HEAD_DIM, :] = vt[h * HEAD_DIM:(h + 1) * HEAD_DIM, :]
        vt_ref[h, HEAD_DIM:V_ROWS, :] = ones_row


def _qkv(x, g_q, g_kv, w_q, w_kvf, b_f, tm=512):
    bsz, seq, d = x.shape
    nh = N_HEADS
    w_k = w_kvf[:, :d]
    w_vt = w_kvf[:, d:2 * d].T
    w_f = w_kvf[:, 2 * d:]
    assert 8 * nh == LANES
    w_f3 = _hi_lo(jnp.tile(w_f, (1, 3)))
    b_f3 = jnp.zeros((1, LANES), F32).at[0, :3 * nh].set(jnp.tile(b_f.astype(F32), 3))

    j = jnp.arange(3)[:, None]
    hh = jnp.arange(nh)[None, :]
    col = lambda i: (hh * AUG + HEAD_DIM + i).reshape(-1)
    src = (j * nh + hh).reshape(-1)
    selq = jnp.zeros((LANES, nh * AUG), BF16).at[src, col(j)].set(1)
    selk = jnp.zeros((LANES, nh * AUG), BF16).at[src, col(3 + j)].set(1)
    cq = jnp.zeros((1, nh * AUG), F32).at[0, col(3 + j)].set(-1.0)
    ck = jnp.zeros((1, nh * AUG), F32).at[0, col(j)].set(1.0)

    const = lambda shape: pl.BlockSpec(shape, lambda b, s: (0,) * len(shape))
    return pl.pallas_call(
        _qkv_kernel,
        grid=(bsz, seq // tm),
        in_specs=[
            pl.BlockSpec((None, tm, d), lambda b, s: (b, s, 0)),
            const((1, d)), const((1, d)), const((d, d)), const((d, d)), const((d, d)), const((d, 2 * LANES)),
            const((1, LANES)), const((LANES, nh * AUG)), const((LANES, nh * AUG)), const((1, nh * AUG)),
            const((1, nh * AUG)),
        ],
        out_specs=[
            pl.BlockSpec((None, nh // 2, 2 * AUG, tm), lambda b, s: (b, 0, 0, s)),
            pl.BlockSpec((None, nh // 2, tm, 2 * AUG), lambda b, s: (b, 0, s, 0)),
            pl.BlockSpec((None, nh, V_ROWS, tm), lambda b, s: (b, 0, 0, s)),
        ],
        out_shape=[
            jax.ShapeDtypeStruct((bsz, nh // 2, 2 * AUG, seq), BF16),
            jax.ShapeDtypeStruct((bsz, nh // 2, seq, 2 * AUG), BF16),
            jax.ShapeDtypeStruct((bsz, nh, V_ROWS, seq), BF16),
        ],
        scratch_shapes=[pltpu.VMEM((SUBLANES, LANES), F32)],
        compiler_params=_params(("parallel", "arbitrary"), 56),
        name="qkv",
    )(x, g_q.reshape(1, d).astype(F32), g_kv.reshape(1, d).astype(F32), w_q.astype(BF16), w_k.astype(BF16),
      w_vt.astype(BF16), w_f3, b_f3, selq, selk, cq, ck)


def _pair_trips(n_pairs, pair, carry):
    for n in (4, 2, 1):
        trip = lambda i, c, n=n: functools.reduce(lambda c, k: pair(n * i + k, c), range(n), c)
        carry = lax.fori_loop(n_pairs // (2 * n) * 2 if n < 4 else 0, n_pairs // n, trip, carry)
    return carry


def _attn_kernel(qt_ref, ka_ref, vt_ref, o_ref, s_even, s_odd, acc_ref):
    tq = qt_ref.shape[1]
    tk = s_even.shape[1]
    qi = pl.program_id(2)
    heads = range(2)
    qs = [qt_ref[hl * AUG:(hl + 1) * AUG, :] for hl in heads]

    def rows(blk):
        return pl.ds(pl.multiple_of(blk * tk, tk), tk)

    def scores(blk, hl):
        return _dot(ka_ref[rows(blk), hl * AUG:(hl + 1) * AUG], qs[hl])

    def causal(s, first_key):
        key = first_key + lax.broadcasted_iota(jnp.int32, s.shape, 0)
        qry = lax.broadcasted_iota(jnp.int32, s.shape, 1)
        return jnp.where(key <= qry, s, -jnp.inf)

    def qk(blk, s_ref):
        for hl in heads:
            s_ref[hl] = scores(blk, hl)

    def consume(blk, s_ref, ms, first_key=None):
        out = []
        for hl in heads:
            s = s_ref[hl] if first_key is None else causal(s_ref[hl], first_key)
            m_new = jnp.maximum(ms[hl], jnp.max(s, axis=0, keepdims=True))
            p = jnp.exp2(s - m_new).astype(BF16)
            acc_ref[hl] = jnp.exp2(ms[hl] - m_new) * acc_ref[hl] + _dot(vt_ref[hl, :, rows(blk)], p)
            out.append(m_new)
        return tuple(out)

    def pair(j, ms):
        qk(2 * j + 1, s_odd)
        ms = consume(2 * j, s_even, ms)
        qk(2 * j + 2, s_even)
        return consume(2 * j + 1, s_odd, ms)

    acc_ref[...] = jnp.zeros_like(acc_ref)
    m_init = jnp.full((1, tq), -jnp.inf, F32)
    qk(0, s_even)
    ms = _pair_trips(qi, pair, (m_init, m_init))
    qk(2 * qi + 1, s_odd)
    ms = consume(2 * qi, s_even, ms, first_key=0)
    consume(2 * qi + 1, s_odd, ms, first_key=tk)

    outs = [acc_ref[hl, 0:HEAD_DIM, :] / acc_ref[hl, HEAD_DIM:HEAD_DIM + 1, :] for hl in heads]
    o_ref[...] = jnp.concatenate(outs, axis=0).T.astype(o_ref.dtype)


def _attention(qt, ka, vt, tq=512):
    bsz, _, seq, _ = ka.shape
    nh = N_HEADS
    tk = tq // 2
    return pl.pallas_call(
        _attn_kernel,
        grid=(bsz, nh // 2, seq // tq),
        in_specs=[
            pl.BlockSpec((None, None, 2 * AUG, tq), lambda b, h, i: (b, h, 0, i)),
            pl.BlockSpec((None, None, seq, 2 * AUG), lambda b, h, i: (b, h, 0, 0)),
            pl.BlockSpec((None, 2, V_ROWS, seq), lambda b, h, i: (b, h, 0, 0)),
        ],
        out_specs=pl.BlockSpec((None, None, tq, 2 * HEAD_DIM), lambda b, h, i: (b, h, i, 0)),
        out_shape=jax.ShapeDtypeStruct((bsz, nh // 2, seq, 2 * HEAD_DIM), BF16),
        scratch_shapes=[pltpu.VMEM((2, tk, tq), F32), pltpu.VMEM((2, tk, tq), F32),
                        pltpu.VMEM((2, V_ROWS, tq), F32)],
        compiler_params=_params(("parallel", "parallel", "arbitrary"), 48),
        name="attn",
    )(qt, ka, vt)


def _oproj_kernel(x_ref, o_ref, wo_ref, g_ref, wr_ref, br_ref, x3_ref, h_ref, gates_ref, route_ref, cnt_ref, count):
    tm = x_ref.shape[0]

    @pl.when(pl.program_id(0) == 0)
    def _():
        count[...] = jnp.zeros_like(count)

    o = jnp.concatenate([o_ref[p] for p in range(o_ref.shape[0])], axis=1)
    x3 = x_ref[...] + _dot(o, wo_ref[...])
    x3_ref[...] = x3
    h = _rms(x3, g_ref[...])
    _store_token_major(h_ref, h)
    logits = _dot_split(h, wr_ref) + br_ref[...]
    lane = lax.broadcasted_iota(jnp.int32, logits.shape, 1)
    m1 = jnp.max(logits, axis=-1, keepdims=True)
    i1 = jnp.min(jnp.where(logits == m1, lane, LANES), axis=-1, keepdims=True)
    rest = jnp.where(lane == i1, -jnp.inf, logits)
    m2 = jnp.max(rest, axis=-1, keepdims=True)
    i2 = jnp.min(jnp.where(rest == m2, lane, LANES), axis=-1, keepdims=True)
    e = jnp.exp(m2 - m1)
    den = 1.0 + e
    gates_ref[...] = jnp.where(lane == 0, 1.0 / den, jnp.where(lane == 1, e / den, 0.0))

    hot = (lane == i1) | (lane == i2)
    r_i = lax.broadcasted_iota(jnp.int32, (tm, tm), 0)
    c_i = lax.broadcasted_iota(jnp.int32, (tm, tm), 1)
    rank = _dot((c_i < r_i).astype(BF16), hot.astype(BF16)) + count[0:1, :]
    count[0:1, :] = rank[tm - 1:tm, :] + hot[tm - 1:tm, :].astype(F32)
    cnt_ref[...] = count[...]
    rank0 = jnp.sum(jnp.where(lane == i1, rank, 0.0), axis=-1, keepdims=True)
    rank1 = jnp.sum(jnp.where(lane == i2, rank, 0.0), axis=-1, keepdims=True)
    packed = jnp.where(lane == 0, rank0, jnp.where(lane == 1, rank1, jnp.where(lane == 2, i1.astype(F32),
                                                                             jnp.where(lane == 3, i2.astype(F32), 0.0))))
    route_ref[...] = packed.T[0:SUBLANES, :].astype(jnp.int32)


def _oproj_router(x, o, w_o, g, router_w, router_b, tm):
    t, d = x.shape
    ne = router_w.shape[-1]
    br = jnp.full((1, LANES), NEG_BIG, F32).at[0, :ne].set(router_b.astype(F32))
    const = lambda shape: pl.BlockSpec(shape, lambda i: (0,) * len(shape))
    _, n_pairs, seq, pair_w = o.shape
    assert seq % tm == 0
    per_seq = seq // tm
    return pl.pallas_call(
        _oproj_kernel,
        grid=(t // tm,),
        in_specs=[pl.BlockSpec((tm, d), lambda i: (i, 0)),
                  pl.BlockSpec((None, n_pairs, tm, pair_w), lambda i: (i // per_seq, 0, i % per_seq, 0)),
                  const((d, d)), const((1, d)), const((d, 2 * LANES)), const((1, LANES))],
        out_specs=[pl.BlockSpec((tm, d), lambda i: (i, 0)), pl.BlockSpec((tm * SUBLANES, LANES), lambda i: (i, 0)),
                   pl.BlockSpec((tm, LANES), lambda i: (i, 0)),
                   pl.BlockSpec((None, SUBLANES, tm), lambda i: (i, 0, 0)),
                   const((SUBLANES, LANES))],
        out_shape=[jax.ShapeDtypeStruct((t, d), F32), jax.ShapeDtypeStruct((t * SUBLANES, LANES), F32),
                   jax.ShapeDtypeStruct((t, LANES), F32),
                   jax.ShapeDtypeStruct((t // tm, SUBLANES, tm), jnp.int32),
                   jax.ShapeDtypeStruct((SUBLANES, LANES), F32)],
        scratch_shapes=[pltpu.VMEM((SUBLANES, LANES), F32)],
        compiler_params=_params(("arbitrary",), 40),
        name="oproj_router",
    )(x, o, w_o.astype(BF16), g.reshape(1, d).astype(F32), _hi_lo(router_w), br)


def _token_copies(copy_of, pos_ref, sem, n_tokens):
    def issue(r, carry):
        for slot in range(TOP_K):
            pltpu.make_async_copy(*copy_of(r, pos_ref[slot, r], slot), sem).start(priority=slot % 2)
        return carry

    lax.fori_loop(0, n_tokens, issue, 0, unroll=8)


def _zero_fill(zeros, xs_ref, sem, first_ref, last_ref, n_spans, tm):
    bits = [1 << k for k in reversed(range(tm.bit_length() - 1))]

    def each(act):
        for i in range(n_spans):
            a = first_ref[i]
            length = last_ref[i] - a
            lax.fori_loop(0, length // tm, lambda j, c: act(a + j * tm, tm) or c, 0)
            done = length // tm * tm
            for nb in bits:
                @pl.when(length & nb != 0)
                def _():
                    act(a + done + (length % tm) // (2 * nb) * (2 * nb), nb)

    copy = lambda p, n: pltpu.make_async_copy(_token(zeros, 0, n), _token(xs_ref, p, n), sem)
    each(lambda p, n: copy(p, n).start())
    each(lambda p, n: copy(p, n).wait())


def _dispatch_kernel(n_spans, first_ref, last_ref, pos_ref, h_ref, xs_ref, zeros, sem):
    tm = h_ref.shape[0] // SUBLANES
    _token_copies(lambda r, p, slot: (_token(h_ref, r), _token(xs_ref, p)), pos_ref, sem, tm)
    for slot in range(TOP_K):
        pltpu.make_async_copy(h_ref, _token(xs_ref, 0, tm), sem).wait()

    @pl.when(pl.program_id(0) == pl.num_programs(0) - 1)
    def _():
        zeros[...] = jnp.zeros_like(zeros)
        _zero_fill(zeros, xs_ref, sem, first_ref, last_ref, n_spans, tm)


def _dispatch(h, pos, span_first, span_last, n_tokens, tm):
    assert tm & (tm - 1) == 0
    grid_spec = pltpu.PrefetchScalarGridSpec(
        num_scalar_prefetch=2,
        grid=(h.shape[0] // (tm * SUBLANES),),
        in_specs=[pl.BlockSpec((None, SUBLANES, tm), lambda i, a, b: (i, 0, 0), memory_space=pltpu.SMEM),
                  pl.BlockSpec((tm * SUBLANES, LANES), lambda i, a, b: (i, 0))],
        out_specs=pl.BlockSpec(memory_space=pl.ANY),
        scratch_shapes=[pltpu.VMEM((tm * SUBLANES, LANES), F32), pltpu.SemaphoreType.DMA],
    )
    return pl.pallas_call(
        functools.partial(_dispatch_kernel, span_first.shape[0]),
        grid_spec=grid_spec,
        out_shape=jax.ShapeDtypeStruct((n_tokens * SUBLANES, LANES), F32),
        compiler_params=_params(("arbitrary",), 24),
        name="moe_dispatch",
    )(span_first, span_last, pos, h)


def _experts_kernel(te_ref, nt_ref, x_ref, wg_ref, wu_ref, wd_ref, y_ref):
    tm = x_ref.shape[0] // SUBLANES
    live = pl.program_id(0) < nt_ref[0]

    @pl.when(live)
    def _():
        x = _load_token_major(x_ref, tm).astype(BF16)
        _store_token_major(y_ref, _swiglu(x, wg_ref, wu_ref, wd_ref, 11))

    @pl.when(jnp.logical_not(live))
    def _():
        y_ref[...] = jnp.zeros_like(y_ref)


def _experts(xs, tile_expert, n_tiles, w_gate, w_up, w_down, tm):
    ne, d, f = w_gate.shape
    expert = lambda g, te, nt: (te[g], 0, 0)
    rows = pl.BlockSpec((tm * SUBLANES, LANES), lambda g, te, nt: (g, 0))
    grid_spec = pltpu.PrefetchScalarGridSpec(
        num_scalar_prefetch=2,
        grid=(xs.shape[0] // (tm * SUBLANES),),
        in_specs=[rows, pl.BlockSpec((None, d, f), expert), pl.BlockSpec((None, d, f), expert),
                  pl.BlockSpec((None, f, d), expert)],
        out_specs=rows,
    )
    return pl.pallas_call(
        _experts_kernel,
        grid_spec=grid_spec,
        out_shape=jax.ShapeDtypeStruct(xs.shape, F32),
        compiler_params=_params(("arbitrary",), 60),
        name="moe_experts",
    )(tile_expert, n_tiles, xs, w_gate.astype(BF16), w_up.astype(BF16), w_down.astype(BF16))


def _combine_kernel(pos_ref, nxt_ref, x_ref, gates_ref, gf_ref, ys_ref, o_ref, ybuf, sems):
    tm = x_ref.shape[0]
    i = pl.program_id(0)
    half = lax.rem(i, 2)

    def fetch(p_ref, h):
        _token_copies(lambda r, p, slot: (_token(ys_ref, p), _token(ybuf.at[h, slot], r)), p_ref, sems.at[h], tm)

    @pl.when(i == 0)
    def _():
        fetch(pos_ref, 0)

    @pl.when(i + 1 < pl.num_programs(0))
    def _():
        fetch(nxt_ref, 1 - half)

    for slot in range(TOP_K):
        pltpu.make_async_copy(_token(ys_ref, 0, tm), ybuf.at[half, slot], sems.at[half]).wait()
    gates = gates_ref[...]
    mix = (gates[:, 0:1] * _load_token_major(ybuf.at[half, 0], tm)
           + gates[:, 1:2] * _load_token_major(ybuf.at[half, 1], tm))
    o_ref[...] = _rms(x_ref[...] + mix, gf_ref[...])


def _combine(x, gates, pos, ys, g_final, tm):
    t, d = x.shape
    n = t // tm
    routes = lambda imap: pl.BlockSpec((None, SUBLANES, tm), imap, memory_space=pltpu.SMEM)
    return pl.pallas_call(
        _combine_kernel,
        grid=(n,),
        in_specs=[routes(lambda i: (i, 0, 0)), routes(lambda i: (jnp.minimum(i + 1, n - 1), 0, 0)),
                  pl.BlockSpec((tm, d), lambda i: (i, 0)), pl.BlockSpec((tm, LANES), lambda i: (i, 0)),
                  pl.BlockSpec((1, d), lambda i: (0, 0)), pl.BlockSpec(memory_space=pl.ANY)],
        out_specs=pl.BlockSpec((tm, d), lambda i: (i, 0)),
        out_shape=jax.ShapeDtypeStruct((t, d), F32),
        scratch_shapes=[pltpu.VMEM((2, TOP_K, tm * SUBLANES, LANES), F32), pltpu.SemaphoreType.DMA((2,))],
        compiler_params=_params(("arbitrary",), 40),
        name="moe_combine",
    )(pos, pos, x, gates, g_final.reshape(1, d).astype(F32), ys)


def _moe(x3, h3, gates, route, counts, w_gate, w_up, w_down, g_final, tm):
    t, d = x3.shape
    ne = w_gate.shape[0]
    n_steps = TOP_K * t // tm + ne
    n = counts[0, :ne].astype(jnp.int32)
    tiles = (n + tm - 1) // tm
    first = jnp.cumsum(tiles) - tiles
    n_tiles = jnp.sum(tiles)
    live = jnp.minimum(jnp.arange(n_steps, dtype=jnp.int32), n_tiles - 1)
    tile_expert = (jnp.sum(live[:, None] >= first[None, :], axis=1) - 1).astype(jnp.int32)

    rank, eid = route[:, 0:TOP_K, :], route[:, TOP_K:2 * TOP_K, :]
    pos = jnp.pad(first[eid] * tm + rank, ((0, 0), (0, SUBLANES - TOP_K), (0, 0)))
    span_first = jnp.concatenate([first * tm + n, (n_tiles * tm)[None]]).astype(jnp.int32)
    span_last = jnp.concatenate([(first + tiles) * tm, jnp.full((1,), n_steps * tm)]).astype(jnp.int32)

    xs = _dispatch(h3, pos, span_first, span_last, n_steps * tm, tm)
    ys = _experts(xs, tile_expert, n_tiles.reshape(1).astype(jnp.int32), w_gate, w_up, w_down, tm)
    return _combine(x3, gates, pos, ys, g_final, tm)


def kernel(x, mix_norm, ffn_norm, conv_w_in, conv_b_in, conv_w_dw, conv_b_dw, conv_ln_g, conv_ln_b, conv_w_out,
           conv_b_out, kv_norm, w_kvf, b_f, w_q, w_o, ffn_w_gate, ffn_w_up, ffn_w_down, router_w, router_b,
           moe_w_gate, moe_w_up, moe_w_down, final_norm):
    bsz, seq, d = x.shape
    assert mix_norm.shape[0] == 2 and conv_w_in.shape[0] == 1 and w_q.shape[0] == 1 and moe_w_gate.shape[0] == 1
    assert d == N_HEADS * HEAD_DIM and conv_w_dw.shape[1] == CONV_WIDTH

    x1 = _mixer0(x, mix_norm[0], conv_w_in[0], conv_b_in[0], conv_w_dw[0], conv_b_dw[0], conv_ln_g[0],
                 conv_ln_b[0], conv_w_out[0], conv_b_out[0])
    x2 = _ffn0(x1.reshape(bsz * seq, d), ffn_norm[0], ffn_w_gate[0], ffn_w_up[0], ffn_w_down[0])
    qt, ka, vt = _qkv(x2.reshape(bsz, seq, d), mix_norm[1], kv_norm, w_q[0], w_kvf, b_f)
    o = _attention(qt, ka, vt)
    x3, h3, gates, route, counts = _oproj_router(x2, o, w_o[0], ffn_norm[1], router_w[0],
                                                 router_b[0], MOE_ROWS)
    out = _moe(x3, h3, gates, route, counts, moe_w_gate[0], moe_w_up[0], moe_w_down[0], final_norm, MOE_ROWS)
    return out.reshape(bsz, seq, d)
```

```python
import functools

import jax
import jax.numpy as jnp
from jax import lax
from jax.experimental import pallas as pl
from jax.experimental.pallas import tpu as pltpu

F32 = jnp.float32
BF16 = jnp.bfloat16

EPS = 1e-6
N_HEADS = 16
HEAD_DIM = 64
CONV_WIDTH = 31
TOP_K = 2

LANES = 128
SUBLANES = 8
MIB = 1024 * 1024

HALO = 32
CONV_ROWS = 64
CONV_COLS = 256
AUG = 2 * HEAD_DIM
V_ROWS = HEAD_DIM + 16
MOE_ROWS = 512
NEG_BIG = -1e30
LOG2E = 1.4426950408889634


def _rms(x, g):
    return x * lax.rsqrt(jnp.mean(x * x, axis=-1, keepdims=True) + EPS) * g


def _dot(a, b):
    return jnp.dot(a, b, preferred_element_type=F32)


def _dot_nt(a, b):
    return lax.dot_general(a, b, (((1,), (1,)), ((), ())), preferred_element_type=F32)


def _split3(x):
    p0 = x.astype(BF16).astype(F32)
    r = x - p0
    p1 = r.astype(BF16).astype(F32)
    return p0, p1, (r - p1).astype(BF16).astype(F32)


def _hi_lo(w):
    w = jnp.pad(w.astype(F32), ((0, 0), (0, LANES - w.shape[1])))
    hi = w.astype(BF16)
    return jnp.concatenate([hi, (w - hi.astype(F32)).astype(BF16)], axis=1)


def _dot_split(x, w_ref):
    hi = x.astype(BF16)
    lo = (x - hi.astype(F32)).astype(BF16)
    both = _dot(hi, w_ref[...])
    return both[:, :LANES] + both[:, LANES:] + _dot(lo, w_ref[:, :LANES])


def _store_token_major(ref, x):
    n = x.shape[0]
    assert x.shape[1] == SUBLANES * LANES
    for s in range(SUBLANES):
        ref[pl.ds(s, n, stride=SUBLANES), :] = x[:, s * LANES:(s + 1) * LANES]


def _load_token_major(ref, n):
    return jnp.concatenate([ref[pl.ds(s, n, stride=SUBLANES), :] for s in range(SUBLANES)], axis=1)


def _token(ref, r, n=1):
    return ref.at[pl.ds(pl.multiple_of(r * SUBLANES, SUBLANES), n * SUBLANES)]


def _params(sem, vmem_mib):
    return pltpu.CompilerParams(dimension_semantics=sem, vmem_limit_bytes=vmem_mib * MIB)


def _mixer_kernel(x_ref, g_ref, win_ref, bin_ref, wdw_ref, bdw_ref, lng_ref, lnb_ref, wout_ref, bout_ref,
                  o_ref, ubuf, shifted, cbuf):
    tm, d = x_ref.shape

    @pl.when(pl.program_id(1) == 0)
    def _():
        ubuf[0:HALO, :] = jnp.zeros((HALO, d), F32)

    x = x_ref[...]
    h = _rms(x, g_ref[...]).astype(BF16)
    u = _dot(h, win_ref[...]) + bin_ref[...]
    ubuf[HALO:HALO + tm, :] = u[:, :d] * jax.nn.sigmoid(u[:, d:])

    n_sh = tm + HALO - SUBLANES
    for b in range(1, SUBLANES):
        shifted[b - 1, 0:n_sh, :] = ubuf[b:b + n_sh, :]

    first = HALO - (CONV_WIDTH - 1)

    n_blk = CONV_ROWS // SUBLANES
    phases = [[(k, (first + k) // SUBLANES) for k in range(CONV_WIDTH) if (first + k) % SUBLANES == b]
              for b in range(SUBLANES)]

    def conv_rows(i, carry):
        r0 = pl.multiple_of(i * CONV_ROWS, CONV_ROWS)
        for c in range(d // CONV_COLS):
            cols = slice(c * CONV_COLS, (c + 1) * CONV_COLS)
            accs = [jnp.zeros((SUBLANES, CONV_COLS), F32) for _ in range(n_blk)]
            for b, taps in enumerate(phases):
                src = ubuf if b == 0 else shifted.at[b - 1]
                wts = {k: jnp.broadcast_to(wdw_ref[k:k + 1, cols], (SUBLANES, CONV_COLS)) for k, _ in taps}
                for j in range(min(a for _, a in taps), n_blk + max(a for _, a in taps)):
                    win = src[pl.ds(r0 + j * SUBLANES, SUBLANES), cols]
                    for k, a in taps:
                        if 0 <= j - a < n_blk:
                            accs[j - a] = accs[j - a] + win * wts[k]
            for t in range(n_blk):
                cbuf[pl.ds(r0 + t * SUBLANES, SUBLANES), cols] = accs[t]
        return carry

    lax.fori_loop(0, tm // CONV_ROWS, conv_rows, 0)

    ubuf[0:HALO, :] = ubuf[tm:tm + HALO, :]

    c = cbuf[...] + bdw_ref[...]
    mu = jnp.mean(c, axis=-1, keepdims=True)
    cc = c - mu
    var = jnp.mean(cc * cc, axis=-1, keepdims=True)
    y = cc * lax.rsqrt(var + EPS) * lng_ref[...] + lnb_ref[...]
    y = (y * jax.nn.sigmoid(y)).astype(BF16)
    o_ref[...] = x + _dot(y, wout_ref[...]) + bout_ref[...]


def _mixer0(x, g, w_in, b_in, w_dw, b_dw, ln_g, ln_b, w_out, b_out, tm=512):
    bsz, seq, d = x.shape
    row = lambda a: a.reshape(1, -1).astype(F32)
    const = lambda shape: pl.BlockSpec(shape, lambda b, s: (0,) * len(shape))
    return pl.pallas_call(
        _mixer_kernel,
        grid=(bsz, seq // tm),
        in_specs=[
            pl.BlockSpec((None, tm, d), lambda b, s: (b, s, 0)),
            const((1, d)), const((d, 2 * d)), const((1, 2 * d)), const((CONV_WIDTH, d)), const((1, d)),
            const((1, d)), const((1, d)), const((d, d)), const((1, d)),
        ],
        out_specs=pl.BlockSpec((None, tm, d), lambda b, s: (b, s, 0)),
        out_shape=jax.ShapeDtypeStruct((bsz, seq, d), F32),
        scratch_shapes=[
            pltpu.VMEM((tm + HALO, d), F32),
            pltpu.VMEM((SUBLANES - 1, tm + HALO - SUBLANES, d), F32),
            pltpu.VMEM((tm, d), F32),
        ],
        compiler_params=_params(("parallel", "arbitrary"), 56),
        name="mixer0",
    )(x, row(g), w_in.astype(BF16), row(b_in), w_dw.astype(F32), row(b_dw), row(ln_g), row(ln_b),
      w_out.astype(BF16), row(b_out))


def _swiglu(h, wg_ref, wu_ref, wd_ref, n_chunks):
    f = wg_ref.shape[-1]
    fc = f // n_chunks
    y = None
    for c in range(n_chunks):
        cols = slice(c * fc, (c + 1) * fc)
        g = _dot(h, wg_ref[:, cols])
        u = _dot(h, wu_ref[:, cols])
        a = (g * jax.nn.sigmoid(g) * u).astype(BF16)
        part = _dot(a, wd_ref[cols, :])
        y = part if y is None else y + part
    return y


def _ffn_kernel(x_ref, g_ref, wg_ref, wu_ref, wd_ref, o_ref):
    x = x_ref[...]
    h = _rms(x, g_ref[...]).astype(BF16)
    o_ref[...] = x + _swiglu(h, wg_ref, wu_ref, wd_ref, 1)


def _ffn0(x, g, w_gate, w_up, w_down, tm=512):
    t, d = x.shape
    f = w_gate.shape[-1]
    const = lambda shape: pl.BlockSpec(shape, lambda i: (0,) * len(shape), pipeline_mode=pl.Buffered(1))
    return pl.pallas_call(
        _ffn_kernel,
        grid=(t // tm,),
        in_specs=[pl.BlockSpec((tm, d), lambda i: (i, 0)), const((1, d)), const((d, f)), const((d, f)),
                  const((f, d))],
        out_specs=pl.BlockSpec((tm, d), lambda i: (i, 0)),
        out_shape=jax.ShapeDtypeStruct((t, d), F32),
        compiler_params=_params(("parallel",), 56),
        name="ffn0",
    )(x, g.reshape(1, d).astype(F32), w_gate.astype(BF16), w_up.astype(BF16), w_down.astype(BF16))


def _qkv_kernel(x_ref, gq_ref, gkv_ref, wq_ref, wk_ref, wvt_ref, wf_ref, bf_ref, selq_ref, selk_ref, cq_ref,
                ck_ref, qt_ref, ka_ref, vt_ref, carry):
    tm, d = x_ref.shape

    @pl.when(pl.program_id(1) == 0)
    def _():
        carry[...] = jnp.zeros_like(carry)

    x = x_ref[...]
    xn = x * lax.rsqrt(jnp.mean(x * x, axis=-1, keepdims=True) + EPS)
    hq = (xn * gq_ref[...]).astype(BF16)
    hkv32 = xn * gkv_ref[...]
    hkv = hkv32.astype(BF16)

    f_logit = _dot_split(hkv32, wf_ref) + bf_ref[...]
    logsig = jnp.minimum(f_logit, 0.0) - jnp.log1p(jnp.exp(-jnp.abs(f_logit)))

    lane = lax.broadcasted_iota(jnp.int32, logsig.shape, 1)
    by_group = lambda p: jnp.where(lane < N_HEADS, p[0], jnp.where(lane < 2 * N_HEADS, p[1],
                                                                   jnp.where(lane < 3 * N_HEADS, p[2], 0.0)))
    r_i = lax.broadcasted_iota(jnp.int32, (tm, tm), 0)
    c_i = lax.broadcasted_iota(jnp.int32, (tm, tm), 1)
    tri = (c_i <= r_i).astype(BF16)
    cum = _dot(tri, by_group(_split3(logsig)).astype(BF16))
    for shift in (N_HEADS, 2 * N_HEADS, 4 * N_HEADS):
        cum = cum + pltpu.roll(cum, shift, axis=1)
    dcum = cum + carry[0:1, :]
    carry[0:1, :] = dcum[tm - 1:tm, :]

    pieces = by_group(_split3(dcum * LOG2E)).astype(BF16)
    xq = _dot(pieces, selq_ref[...]) + cq_ref[...]
    xk = _dot(pieces, selk_ref[...]) + ck_ref[...]

    q = _dot(hq, wq_ref[...]) * (HEAD_DIM ** -0.5 * LOG2E)
    k = _dot(hkv, wk_ref[...])
    lane_a = lax.broadcasted_iota(jnp.int32, (tm, AUG), 1)
    for p in range(N_HEADS // 2):
        src = slice(p * AUG, (p + 1) * AUG)
        q_pair = []
        for odd in range(2):
            dst = slice((2 * p + odd) * AUG, (2 * p + odd + 1) * AUG)
            qv, kv = q[:, src], k[:, src]
            if odd:
                qv, kv = pltpu.roll(qv, HEAD_DIM, axis=1), pltpu.roll(kv, HEAD_DIM, axis=1)
            out = slice(odd * AUG, (odd + 1) * AUG)
            q_pair.append(jnp.where(lane_a < HEAD_DIM, qv, xq[:, dst]))
            ka_ref[p, :, out] = jnp.where(lane_a < HEAD_DIM, kv, xk[:, dst]).astype(BF16)
        qt_ref[p] = jnp.concatenate(q_pair, axis=1).T.astype(BF16)

    vt = _dot_nt(wvt_ref[...], hkv).astype(BF16)
    row = lax.broadcasted_iota(jnp.int32, (V_ROWS - HEAD_DIM, tm), 0)
    ones_row = (row == 0).astype(BF16)
    for h in range(N_HEADS):
        vt_ref[h, 0:HEAD_DIM, :] = vt[h * HEAD_DIM:(h + 1) * HEAD_DIM, :]
        vt_ref[h, HEAD_DIM:V_ROWS, :] = ones_row


def _qkv(x, g_q, g_kv, w_q, w_kvf, b_f, tm=512):
    bsz, seq, d = x.shape
    nh = N_HEADS
    w_k = w_kvf[:, :d]
    w_vt = w_kvf[:, d:2 * d].T
    w_f = w_kvf[:, 2 * d:]
    assert 8 * nh == LANES
    w_f3 = _hi_lo(jnp.tile(w_f, (1, 3)))
    b_f3 = jnp.zeros((1, LANES), F32).at[0, :3 * nh].set(jnp.tile(b_f.astype(F32), 3))

    j = jnp.arange(3)[:, None]
    hh = jnp.arange(nh)[None, :]
    col = lambda i: (hh * AUG + HEAD_DIM + i).reshape(-1)
    src = (j * nh + hh).reshape(-1)
    selq = jnp.zeros((LANES, nh * AUG), BF16).at[src, col(j)].set(1)
    selk = jnp.zeros((LANES, nh * AUG), BF16).at[src, col(3 + j)].set(1)
    cq = jnp.zeros((1, nh * AUG), F32).at[0, col(3 + j)].set(-1.0)
    ck = jnp.zeros((1, nh * AUG), F32).at[0, col(j)].set(1.0)

    const = lambda shape: pl.BlockSpec(shape, lambda b, s: (0,) * len(shape))
    return pl.pallas_call(
        _qkv_kernel,
        grid=(bsz, seq // tm),
        in_specs=[
            pl.BlockSpec((None, tm, d), lambda b, s: (b, s, 0)),
            const((1, d)), const((1, d)), const((d, d)), const((d, d)), const((d, d)), const((d, 2 * LANES)),
            const((1, LANES)), const((LANES, nh * AUG)), const((LANES, nh * AUG)), const((1, nh * AUG)),
            const((1, nh * AUG)),
        ],
        out_specs=[
            pl.BlockSpec((None, nh // 2, 2 * AUG, tm), lambda b, s: (b, 0, 0, s)),
            pl.BlockSpec((None, nh // 2, tm, 2 * AUG), lambda b, s: (b, 0, s, 0)),
            pl.BlockSpec((None, nh, V_ROWS, tm), lambda b, s: (b, 0, 0, s)),
        ],
        out_shape=[
            jax.ShapeDtypeStruct((bsz, nh // 2, 2 * AUG, seq), BF16),
            jax.ShapeDtypeStruct((bsz, nh // 2, seq, 2 * AUG), BF16),
            jax.ShapeDtypeStruct((bsz, nh, V_ROWS, seq), BF16),
        ],
        scratch_shapes=[pltpu.VMEM((SUBLANES, LANES), F32)],
        compiler_params=_params(("parallel", "arbitrary"), 56),
        name="qkv",
    )(x, g_q.reshape(1, d).astype(F32), g_kv.reshape(1, d).astype(F32), w_q.astype(BF16), w_k.astype(BF16),
      w_vt.astype(BF16), w_f3, b_f3, selq, selk, cq, ck)


def _pair_trips(n_pairs, pair, carry):
    for n in (4, 2, 1):
        trip = lambda i, c, n=n: functools.reduce(lambda c, k: pair(n * i + k, c), range(n), c)
        carry = lax.fori_loop(n_pairs // (2 * n) * 2 if n < 4 else 0, n_pairs // n, trip, carry)
    return carry


def _attn_kernel(qt_ref, ka_ref, vt_ref, o_ref, s_even, s_odd, acc_ref):
    _, tk, tq = s_even.shape
    heads = range(2)

    def rows(blk):
        return pl.ds(pl.multiple_of(blk * tk, tk), tk)

    def causal(s, first_key):
        key = first_key + lax.broadcasted_iota(jnp.int32, s.shape, 0)
        qry = lax.broadcasted_iota(jnp.int32, s.shape, 1)
        return jnp.where(key <= qry, s, -jnp.inf)

    def tile(qi, carry):
        queries = pl.ds(pl.multiple_of(qi * tq, tq), tq)
        qs = [qt_ref[hl * AUG:(hl + 1) * AUG, queries] for hl in heads]

        def qk(blk, s_ref):
            for hl in heads:
                s_ref[hl] = _dot(ka_ref[rows(blk), hl * AUG:(hl + 1) * AUG], qs[hl])

        def consume(blk, s_ref, ms, first_key=None):
            out = []
            for hl in heads:
                s = s_ref[hl] if first_key is None else causal(s_ref[hl], first_key)
                m_new = jnp.maximum(ms[hl], jnp.max(s, axis=0, keepdims=True))
                p = jnp.exp2(s - m_new).astype(BF16)
                acc_ref[hl] = jnp.exp2(ms[hl] - m_new) * acc_ref[hl] + _dot(vt_ref[hl, :, rows(blk)], p)
                out.append(m_new)
            return tuple(out)

        def pair(j, ms):
            qk(2 * j + 1, s_odd)
            ms = consume(2 * j, s_even, ms)
            qk(2 * j + 2, s_even)
            return consume(2 * j + 1, s_odd, ms)

        acc_ref[...] = jnp.zeros_like(acc_ref)
        m_init = jnp.full((1, tq), -jnp.inf, F32)
        qk(0, s_even)
        ms = _pair_trips(qi, pair, (m_init, m_init))
        qk(2 * qi + 1, s_odd)
        ms = consume(2 * qi, s_even, ms, first_key=0)
        consume(2 * qi + 1, s_odd, ms, first_key=tk)

        outs = [acc_ref[hl, 0:HEAD_DIM, :] / acc_ref[hl, HEAD_DIM:HEAD_DIM + 1, :] for hl in heads]
        o_ref[queries, :] = jnp.concatenate(outs, axis=0).T.astype(o_ref.dtype)
        return carry

    lax.fori_loop(0, o_ref.shape[0] // tq, tile, 0)


def _attention(qt, ka, vt, tq=512):
    bsz, _, seq, _ = ka.shape
    nh = N_HEADS
    tk = tq // 2
    return pl.pallas_call(
        _attn_kernel,
        grid=(bsz, nh // 2),
        in_specs=[
            pl.BlockSpec((None, None, 2 * AUG, seq), lambda b, h: (b, h, 0, 0)),
            pl.BlockSpec((None, None, seq, 2 * AUG), lambda b, h: (b, h, 0, 0)),
            pl.BlockSpec((None, 2, V_ROWS, seq), lambda b, h: (b, h, 0, 0)),
        ],
        out_specs=pl.BlockSpec((None, None, seq, 2 * HEAD_DIM), lambda b, h: (b, h, 0, 0)),
        out_shape=jax.ShapeDtypeStruct((bsz, nh // 2, seq, 2 * HEAD_DIM), BF16),
        scratch_shapes=[pltpu.VMEM((2, tk, tq), F32), pltpu.VMEM((2, tk, tq), F32),
                        pltpu.VMEM((2, V_ROWS, tq), F32)],
        compiler_params=_params(("parallel", "parallel"), 48),
        name="attn",
    )(qt, ka, vt)


def _oproj_kernel(x_ref, o_ref, wo_ref, g_ref, wr_ref, br_ref, x3_ref, h_ref, gates_ref, route_ref, cnt_ref, count):
    tm = x_ref.shape[0]

    @pl.when(pl.program_id(0) == 0)
    def _():
        count[...] = jnp.zeros_like(count)

    o = jnp.concatenate([o_ref[p] for p in range(o_ref.shape[0])], axis=1)
    x3 = x_ref[...] + _dot(o, wo_ref[...])
    x3_ref[...] = x3
    h = _rms(x3, g_ref[...])
    _store_token_major(h_ref, h)
    logits = _dot_split(h, wr_ref) + br_ref[...]
    lane = lax.broadcasted_iota(jnp.int32, logits.shape, 1)
    m1 = jnp.max(logits, axis=-1, keepdims=True)
    i1 = jnp.min(jnp.where(logits == m1, lane, LANES), axis=-1, keepdims=True)
    rest = jnp.where(lane == i1, -jnp.inf, logits)
    m2 = jnp.max(rest, axis=-1, keepdims=True)
    i2 = jnp.min(jnp.where(rest == m2, lane, LANES), axis=-1, keepdims=True)
    e = jnp.exp(m2 - m1)
    den = 1.0 + e
    gates_ref[...] = jnp.where(lane == 0, 1.0 / den, jnp.where(lane == 1, e / den, 0.0))

    hot = (lane == i1) | (lane == i2)
    r_i = lax.broadcasted_iota(jnp.int32, (tm, tm), 0)
    c_i = lax.broadcasted_iota(jnp.int32, (tm, tm), 1)
    rank = _dot((c_i < r_i).astype(BF16), hot.astype(BF16)) + count[0:1, :]
    count[0:1, :] = rank[tm - 1:tm, :] + hot[tm - 1:tm, :].astype(F32)
    cnt_ref[...] = count[...]
    rank0 = jnp.sum(jnp.where(lane == i1, rank, 0.0), axis=-1, keepdims=True)
    rank1 = jnp.sum(jnp.where(lane == i2, rank, 0.0), axis=-1, keepdims=True)
    packed = jnp.where(lane == 0, rank0, jnp.where(lane == 1, rank1, jnp.where(lane == 2, i1.astype(F32),
                                                                             jnp.where(lane == 3, i2.astype(F32), 0.0))))
    route_ref[...] = packed.T[0:SUBLANES, :].astype(jnp.int32)


def _oproj_router(x, o, w_o, g, router_w, router_b, tm):
    t, d = x.shape
    ne = router_w.shape[-1]
    br = jnp.full((1, LANES), NEG_BIG, F32).at[0, :ne].set(router_b.astype(F32))
    const = lambda shape: pl.BlockSpec(shape, lambda i: (0,) * len(shape))
    _, n_pairs, seq, pair_w = o.shape
    assert seq % tm == 0
    per_seq = seq // tm
    return pl.pallas_call(
        _oproj_kernel,
        grid=(t // tm,),
        in_specs=[pl.BlockSpec((tm, d), lambda i: (i, 0)),
                  pl.BlockSpec((None, n_pairs, tm, pair_w), lambda i: (i // per_seq, 0, i % per_seq, 0)),
                  const((d, d)), const((1, d)), const((d, 2 * LANES)), const((1, LANES))],
        out_specs=[pl.BlockSpec((tm, d), lambda i: (i, 0)), pl.BlockSpec((tm * SUBLANES, LANES), lambda i: (i, 0)),
                   pl.BlockSpec((tm, LANES), lambda i: (i, 0)),
                   pl.BlockSpec((None, SUBLANES, tm), lambda i: (i, 0, 0)),
                   const((SUBLANES, LANES))],
        out_shape=[jax.ShapeDtypeStruct((t, d), F32), jax.ShapeDtypeStruct((t * SUBLANES, LANES), F32),
                   jax.ShapeDtypeStruct((t, LANES), F32),
                   jax.ShapeDtypeStruct((t // tm, SUBLANES, tm), jnp.int32),
                   jax.ShapeDtypeStruct((SUBLANES, LANES), F32)],
        scratch_shapes=[pltpu.VMEM((SUBLANES, LANES), F32)],
        compiler_params=_params(("arbitrary",), 40),
        name="oproj_router",
    )(x, o, w_o.astype(BF16), g.reshape(1, d).astype(F32), _hi_lo(router_w), br)


def _token_copies(copy_of, pos_ref, sem, n_tokens):
    def issue(r, carry):
        for slot in range(TOP_K):
            pltpu.make_async_copy(*copy_of(r, pos_ref[slot, r], slot), sem).start(priority=slot % 2)
        return carry

    lax.fori_loop(0, n_tokens, issue, 0, unroll=8)


def _zero_fill(zeros, xs_ref, sem, first_ref, last_ref, n_spans, tm):
    bits = [1 << k for k in reversed(range(tm.bit_length() - 1))]

    def each(act):
        for i in range(n_spans):
            a = first_ref[i]
            length = last_ref[i] - a
            lax.fori_loop(0, length // tm, lambda j, c: act(a + j * tm, tm) or c, 0)
            done = length // tm * tm
            for nb in bits:
                @pl.when(length & nb != 0)
                def _():
                    act(a + done + (length % tm) // (2 * nb) * (2 * nb), nb)

    copy = lambda p, n: pltpu.make_async_copy(_token(zeros, 0, n), _token(xs_ref, p, n), sem)
    each(lambda p, n: copy(p, n).start())
    each(lambda p, n: copy(p, n).wait())


def _dispatch_kernel(n_spans, first_ref, last_ref, pos_ref, h_ref, xs_ref, zeros, sem):
    tm = h_ref.shape[0] // SUBLANES
    _token_copies(lambda r, p, slot: (_token(h_ref, r), _token(xs_ref, p)), pos_ref, sem, tm)
    for slot in range(TOP_K):
        pltpu.make_async_copy(h_ref, _token(xs_ref, 0, tm), sem).wait()

    @pl.when(pl.program_id(0) == pl.num_programs(0) - 1)
    def _():
        zeros[...] = jnp.zeros_like(zeros)
        _zero_fill(zeros, xs_ref, sem, first_ref, last_ref, n_spans, tm)


def _dispatch(h, pos, span_first, span_last, n_tokens, tm):
    assert tm & (tm - 1) == 0
    grid_spec = pltpu.PrefetchScalarGridSpec(
        num_scalar_prefetch=2,
        grid=(h.shape[0] // (tm * SUBLANES),),
        in_specs=[pl.BlockSpec((None, SUBLANES, tm), lambda i, a, b: (i, 0, 0), memory_space=pltpu.SMEM),
                  pl.BlockSpec((tm * SUBLANES, LANES), lambda i, a, b: (i, 0))],
        out_specs=pl.BlockSpec(memory_space=pl.ANY),
        scratch_shapes=[pltpu.VMEM((tm * SUBLANES, LANES), F32), pltpu.SemaphoreType.DMA],
    )
    return pl.pallas_call(
        functools.partial(_dispatch_kernel, span_first.shape[0]),
        grid_spec=grid_spec,
        out_shape=jax.ShapeDtypeStruct((n_tokens * SUBLANES, LANES), F32),
        compiler_params=_params(("arbitrary",), 24),
        name="moe_dispatch",
    )(span_first, span_last, pos, h)


def _experts_kernel(te_ref, nt_ref, x_ref, wg_ref, wu_ref, wd_ref, y_ref):
    tm = x_ref.shape[0] // SUBLANES
    live = pl.program_id(0) < nt_ref[0]

    @pl.when(live)
    def _():
        x = _load_token_major(x_ref, tm).astype(BF16)
        _store_token_major(y_ref, _swiglu(x, wg_ref, wu_ref, wd_ref, 11))

    @pl.when(jnp.logical_not(live))
    def _():
        y_ref[...] = jnp.zeros_like(y_ref)


def _experts(xs, tile_expert, n_tiles, w_gate, w_up, w_down, tm):
    ne, d, f = w_gate.shape
    expert = lambda g, te, nt: (te[g], 0, 0)
    rows = pl.BlockSpec((tm * SUBLANES, LANES), lambda g, te, nt: (g, 0))
    grid_spec = pltpu.PrefetchScalarGridSpec(
        num_scalar_prefetch=2,
        grid=(xs.shape[0] // (tm * SUBLANES),),
        in_specs=[rows, pl.BlockSpec((None, d, f), expert), pl.BlockSpec((None, d, f), expert),
                  pl.BlockSpec((None, f, d), expert)],
        out_specs=rows,
    )
    return pl.pallas_call(
        _experts_kernel,
        grid_spec=grid_spec,
        out_shape=jax.ShapeDtypeStruct(xs.shape, F32),
        compiler_params=_params(("arbitrary",), 60),
        name="moe_experts",
    )(tile_expert, n_tiles, xs, w_gate.astype(BF16), w_up.astype(BF16), w_down.astype(BF16))


def _combine_kernel(pos_ref, nxt_ref, x_ref, gates_ref, gf_ref, ys_ref, o_ref, ybuf, sems):
    tm = x_ref.shape[0]
    i = pl.program_id(0)
    half = lax.rem(i, 2)

    def fetch(p_ref, h):
        _token_copies(lambda r, p, slot: (_token(ys_ref, p), _token(ybuf.at[h, slot], r)), p_ref, sems.at[h], tm)

    @pl.when(i == 0)
    def _():
        fetch(pos_ref, 0)

    @pl.when(i + 1 < pl.num_programs(0))
    def _():
        fetch(nxt_ref, 1 - half)

    for slot in range(TOP_K):
        pltpu.make_async_copy(_token(ys_ref, 0, tm), ybuf.at[half, slot], sems.at[half]).wait()
    gates = gates_ref[...]
    mix = (gates[:, 0:1] * _load_token_major(ybuf.at[half, 0], tm)
           + gates[:, 1:2] * _load_token_major(ybuf.at[half, 1], tm))
    o_ref[...] = _rms(x_ref[...] + mix, gf_ref[...])


def _combine(x, gates, pos, ys, g_final, tm):
    t, d = x.shape
    n = t // tm
    routes = lambda imap: pl.BlockSpec((None, SUBLANES, tm), imap, memory_space=pltpu.SMEM)
    return pl.pallas_call(
        _combine_kernel,
        grid=(n,),
        in_specs=[routes(lambda i: (i, 0, 0)), routes(lambda i: (jnp.minimum(i + 1, n - 1), 0, 0)),
                  pl.BlockSpec((tm, d), lambda i: (i, 0)), pl.BlockSpec((tm, LANES), lambda i: (i, 0)),
                  pl.BlockSpec((1, d), lambda i: (0, 0)), pl.BlockSpec(memory_space=pl.ANY)],
        out_specs=pl.BlockSpec((tm, d), lambda i: (i, 0)),
        out_shape=jax.ShapeDtypeStruct((t, d), F32),
        scratch_shapes=[pltpu.VMEM((2, TOP_K, tm * SUBLANES, LANES), F32), pltpu.SemaphoreType.DMA((2,))],
        compiler_params=_params(("arbitrary",), 40),
        name="moe_combine",
    )(pos, pos, x, gates, g_final.reshape(1, d).astype(F32), ys)


def _moe(x3, h3, gates, route, counts, w_gate, w_up, w_down, g_final, tm):
    t, d = x3.shape
    ne = w_gate.shape[0]
    n_steps = TOP_K * t // tm + ne
    n = counts[0, :ne].astype(jnp.int32)
    tiles = (n + tm - 1) // tm
    first = jnp.cumsum(tiles) - tiles
    n_tiles = jnp.sum(tiles)
    live = jnp.minimum(jnp.arange(n_steps, dtype=jnp.int32), n_tiles - 1)
    tile_expert = (jnp.sum(live[:, None] >= first[None, :], axis=1) - 1).astype(jnp.int32)

    rank, eid = route[:, 0:TOP_K, :], route[:, TOP_K:2 * TOP_K, :]
    pos = jnp.pad(first[eid] * tm + rank, ((0, 0), (0, SUBLANES - TOP_K), (0, 0)))
    span_first = jnp.concatenate([first * tm + n, (n_tiles * tm)[None]]).astype(jnp.int32)
    span_last = jnp.concatenate([(first + tiles) * tm, jnp.full((1,), n_steps * tm)]).astype(jnp.int32)

    xs = _dispatch(h3, pos, span_first, span_last, n_steps * tm, tm)
    ys = _experts(xs, tile_expert, n_tiles.reshape(1).astype(jnp.int32), w_gate, w_up, w_down, tm)
    return _combine(x3, gates, pos, ys, g_final, tm)


def kernel(x, mix_norm, ffn_norm, conv_w_in, conv_b_in, conv_w_dw, conv_b_dw, conv_ln_g, conv_ln_b, conv_w_out,
           conv_b_out, kv_norm, w_kvf, b_f, w_q, w_o, ffn_w_gate, ffn_w_up, ffn_w_down, router_w, router_b,
           moe_w_gate, moe_w_up, moe_w_down, final_norm):
    bsz, seq, d = x.shape
    assert mix_norm.shape[0] == 2 and conv_w_in.shape[0] == 1 and w_q.shape[0] == 1 and moe_w_gate.shape[0] == 1
    assert d == N_HEADS * HEAD_DIM and conv_w_dw.shape[1] == CONV_WIDTH

    x1 = _mixer0(x, mix_norm[0], conv_w_in[0], conv_b_in[0], conv_w_dw[0], conv_b_dw[0], conv_ln_g[0],
                 conv_ln_b[0], conv_w_out[0], conv_b_out[0])
    x2 = _ffn0(x1.reshape(bsz * seq, d), ffn_norm[0], ffn_w_gate[0], ffn_w_up[0], ffn_w_down[0])
    qt, ka, vt = _qkv(x2.reshape(bsz, seq, d), mix_norm[1], kv_norm, w_q[0], w_kvf, b_f)
    o = _attention(qt, ka, vt)
    x3, h3, gates, route, counts = _oproj_router(x2, o, w_o[0], ffn_norm[1], router_w[0],
                                                 router_b[0], MOE_ROWS)
    out = _moe(x3, h3, gates, route, counts, moe_w_gate[0], moe_w_up[0], moe_w_down[0], final_norm, MOE_ROWS)
    return out.reshape(bsz, seq, d)
```
